```python
import math
import jax, jax.numpy as jnp
from jax import lax
import numpy as np

D_MODEL = 1024
BATCH = 16
SEQ = 2048
DEPTH = 1
DEC_BATCH = 32
DEC_SEQ = 8
PAST_LEN = 16384
PAGE_SIZE = 128

HEAD_DIM = 64
N_HEADS_A = 8
N_KV_A = 2
N_IDX_HEADS = 8
IDX_DIM = 64
TOPK_MAX = 256
N_HEADS_B = 4
N_KV_B = 2
N_EXPERTS = 32
TOP_K_EXPERTS = 4
D_FF = 1024
SWIGLU_ALPHA = 1.702
SWIGLU_LIMIT = 7.0
ROPE_THETA = 10000.0
LN_EPS = 1e-5
Q_BLOCK = 128
Q_A = N_HEADS_A * HEAD_DIM
KV_A = N_KV_A * HEAD_DIM
Q_IDX = N_IDX_HEADS * IDX_DIM
Q_B = N_HEADS_B * 2 * HEAD_DIM
KV_B = N_KV_B * 2 * HEAD_DIM
OUT_A = N_HEADS_A * HEAD_DIM
OUT_B = N_HEADS_B * 2 * HEAD_DIM
PROJ_SIZES = (Q_A, KV_A, KV_A, Q_IDX, IDX_DIM, N_IDX_HEADS, Q_B, KV_B, KV_B, D_MODEL, D_MODEL)
D_IN = sum(PROJ_SIZES)
IDX_SCALE = (N_IDX_HEADS * IDX_DIM) ** -0.5
DEEPNORM_ALPHA = (2 * DEPTH) ** 0.25
DEEPNORM_BETA = (8 * DEPTH) ** -0.25

kernel_name = 'hybrid_dsa_diffattn_moe_step'


def _layernorm(x, g=None, b=None):
    xf = x.astype(jnp.float32)
    mu = jnp.mean(xf, axis=-1, keepdims=True)
    var = jnp.mean(jnp.square(xf - mu), axis=-1, keepdims=True)
    y = (xf - mu) * lax.rsqrt(var + LN_EPS)
    if g is not None:
        y = y * g.astype(jnp.float32) + b.astype(jnp.float32)
    return y.astype(x.dtype)


def _rmsnorm(x, g):
    xf = x.astype(jnp.float32)
    y = xf * lax.rsqrt(jnp.mean(jnp.square(xf), axis=-1, keepdims=True) + LN_EPS)
    return (y * g.astype(jnp.float32)).astype(x.dtype)


def _rope(x, pos):
    half = x.shape[-1] // 2
    inv = ROPE_THETA ** (-jnp.arange(half, dtype=jnp.float32) / half)
    ang = pos.astype(jnp.float32)[:, None] * inv[None, :]
    ang = ang.reshape((ang.shape[0],) + (1,) * (x.ndim - 3) + (half,))
    cos, sin = jnp.cos(ang), jnp.sin(ang)
    xf = x.astype(jnp.float32)
    x1, x2 = xf[..., :half], xf[..., half:]
    return jnp.concatenate([x1 * cos - x2 * sin, x2 * cos + x1 * sin], axis=-1).astype(x.dtype)


def _project(h, pos, w_in):
    b, t, _ = h.shape
    offs = [int(o) for o in np.cumsum(PROJ_SIZES)[:-1]]
    qa, ka, va, qi, ki, wi, qb, kb, vb, ga, gb = jnp.split(h @ w_in, offs, axis=-1)
    r_a, r_b = N_HEADS_A // N_KV_A, N_HEADS_B // N_KV_B
    return dict(
        qa=_rope(qa.reshape(b, t, N_KV_A, r_a, HEAD_DIM), pos),
        ka=_rope(ka.reshape(b, t, N_KV_A, HEAD_DIM), pos),
        va=va.reshape(b, t, N_KV_A, HEAD_DIM),
        qi=_rope(qi.reshape(b, t, N_IDX_HEADS, IDX_DIM), pos),
        ki=_rope(ki, pos),
        wi=wi,
        qb=_rope(qb.reshape(b, t, N_KV_B, r_b, 2, HEAD_DIM), pos),
        kb=_rope(kb.reshape(b, t, N_KV_B, 2, HEAD_DIM), pos),
        vb=vb.reshape(b, t, N_KV_B, 2 * HEAD_DIM),
        ga=ga, gb=gb)


def _index_select(qi, wi, ki, q_pos, k_pos, topk):
    s = jnp.einsum('bqhd,bsd->bqhs', qi, ki).astype(jnp.float32)
    score = jnp.einsum('bqh,bqhs->bqs', wi.astype(jnp.float32) * IDX_SCALE, jax.nn.relu(s))
    score = jnp.where((k_pos[None, :] <= q_pos[:, None])[None], score, -jnp.inf)
    _, idx = lax.top_k(score, topk)
    return idx


def _sparse_attend(qa, k_sel, v_sel, valid):
    b, q = qa.shape[:2]
    s = jnp.einsum('bqgrd,bqkgd->bqgrk', qa, k_sel).astype(jnp.float32) * HEAD_DIM ** -0.5
    p = jax.nn.softmax(jnp.where(valid[:, :, None, None, :], s, -jnp.inf), axis=-1)
    o = jnp.einsum('bqgrk,bqkgd->bqgrd', p.astype(v_sel.dtype), v_sel)
    return o.reshape(b, q, OUT_A)


def _diff_attend(qb, kb, vb, mask, lam, lam_init, subln):
    b, q = qb.shape[:2]
    s = jnp.einsum('bqgrcd,bsgcd->bgrcqs', qb, kb).astype(jnp.float32) * HEAD_DIM ** -0.5
    p = jax.nn.softmax(jnp.where(mask, s, -jnp.inf), axis=-1)
    a = p[:, :, :, 0] - lam * p[:, :, :, 1]
    o = jnp.einsum('bgrqs,bsge->bqgre', a.astype(vb.dtype), vb)
    o = _rmsnorm(o, subln) * (1.0 - lam_init)
    return o.reshape(b, q, OUT_B)


def _prompt_mixers(p, lam, lam_init, subln):
    b, t = p['ka'].shape[:2]
    topk = min(TOPK_MAX, t // 4)
    kpos = jnp.arange(t)
    gather = jax.vmap(lambda rows, idx: rows[idx])

    def block(i):
        q0 = i * Q_BLOCK
        sl = lambda a: lax.dynamic_slice_in_dim(a, q0, Q_BLOCK, axis=1)
        qpos = q0 + jnp.arange(Q_BLOCK)
        idx = _index_select(sl(p['qi']), sl(p['wi']), p['ki'], qpos, kpos, topk)
        valid = idx <= qpos[None, :, None]
        oa = _sparse_attend(sl(p['qa']), gather(p['ka'], idx), gather(p['va'], idx), valid)
        ob = _diff_attend(sl(p['qb']), p['kb'], p['vb'], kpos[None, :] <= qpos[:, None], lam, lam_init, subln)
        return oa, ob

    oa, ob = lax.map(block, jnp.arange(t // Q_BLOCK))
    merge = lambda o: jnp.swapaxes(o, 0, 1).reshape(b, t, o.shape[-1])
    return merge(oa), merge(ob)


def _sample_mixers(p, k_a_pool, v_a_pool, k_idx_pool, k_b_pool, v_b_pool, page_table, lam, lam_init, subln):
    b, t = p['ka'].shape[:2]
    n_keys = PAST_LEN + t
    topk = min(TOPK_MAX, n_keys // 4)
    qpos = PAST_LEN + jnp.arange(t)
    kpos = jnp.arange(n_keys)

    def gather_past(pool):
        return pool[page_table].reshape((b, PAST_LEN) + pool.shape[2:])

    ki_all = jnp.concatenate([gather_past(k_idx_pool), p['ki']], axis=1)
    idx = _index_select(p['qi'], p['wi'], ki_all, qpos, kpos, topk)
    valid = idx <= qpos[None, :, None]
    in_past = (idx < PAST_LEN)[..., None, None]
    past_idx = jnp.minimum(idx, PAST_LEN - 1)
    phys = jnp.take_along_axis(page_table, (past_idx // PAGE_SIZE).reshape(b, -1), axis=1).reshape(idx.shape)
    off = past_idx % PAGE_SIZE
    new_idx = jnp.clip(idx - PAST_LEN, 0, t - 1)
    gather_new = jax.vmap(lambda rows, i: rows[i])

    def select(pool, new):
        return jnp.where(in_past, pool[phys, off], gather_new(new, new_idx))

    oa = _sparse_attend(p['qa'], select(k_a_pool, p['ka']), select(v_a_pool, p['va']), valid)
    kb_all = jnp.concatenate([gather_past(k_b_pool), p['kb']], axis=1)
    vb_all = jnp.concatenate([gather_past(v_b_pool), p['vb']], axis=1)
    ob = _diff_attend(p['qb'], kb_all, vb_all, kpos[None, :] <= qpos[:, None], lam, lam_init, subln)
    return oa, ob


def _moe(h, w):
    shp = h.shape
    tok = h.reshape(-1, shp[-1])
    logits = (tok @ w['w_router'] + w['b_router']).astype(jnp.float32)
    top_v, top_i = lax.top_k(logits, TOP_K_EXPERTS)
    top_w = jax.nn.softmax(top_v, axis=-1)
    combine = jnp.einsum('tk,tke->te', top_w, jax.nn.one_hot(top_i, N_EXPERTS, dtype=jnp.float32)).astype(h.dtype)
    out = jnp.zeros_like(tok)
    for e in range(N_EXPERTS):
        gate = jnp.minimum(tok @ w['w_gate'][e] + w['b_gate'][e], SWIGLU_LIMIT)
        up = jnp.clip(tok @ w['w_up'][e] + w['b_up'][e], -SWIGLU_LIMIT, SWIGLU_LIMIT)
        act = (up + 1.0) * gate * jax.nn.sigmoid(SWIGLU_ALPHA * gate)
        out = out + combine[:, e:e + 1] * (act @ w['w_down'][e] + w['b_down'][e])
    return out.reshape(shp)


def _layer(x, c, pos, mix_fn, w):
    mod = (jax.nn.silu(c) @ w['w_ada'] + w['b_ada'])[:, None, :]
    sh1, sc1, g1, sh2, sc2, g2 = jnp.split(mod, 6, axis=-1)
    h = _layernorm(x) * (1.0 + sc1) + sh1
    p = _project(h, pos, w['w_in'])
    oa, ob = mix_fn(p)
    merged = jax.nn.sigmoid(p['ga']) * (oa @ w['w_br_a']) + jax.nn.sigmoid(p['gb']) * (ob @ w['w_br_b'])
    x = _layernorm(DEEPNORM_ALPHA * x + (1.0 + g1) * (merged @ w['w_o']), w['ln1_g'], w['ln1_b'])
    h2 = _layernorm(x) * (1.0 + sc2) + sh2
    x = _layernorm(DEEPNORM_ALPHA * x + (1.0 + g2) * _moe(h2, w), w['ln2_g'], w['ln2_b'])
    state = dict(ka=p['ka'], va=p['va'], ki=p['ki'], kb=p['kb'], vb=p['vb'])
    return x, state


def setup_inputs(seed: int = 0) -> dict:
    key = jax.random.key(seed)
    ks = iter(jax.random.split(key, 48))

    def nrm(shape, scale=1.0):
        return jax.random.normal(next(ks), shape, jnp.float32) * scale

    n_pages = PAST_LEN // PAGE_SIZE
    n_used = DEC_BATCH * n_pages
    n_pool = n_used + (n_used + 3) // 4
    page_table = jax.random.permutation(next(ks), n_pool)[:n_used].reshape(DEC_BATCH, n_pages).astype(jnp.int32)
    pool = (DEPTH, n_pool, PAGE_SIZE)
    L = DEPTH
    fan = lambda n: n ** -0.5
    return {
        'x_prompt': nrm((BATCH, SEQ, D_MODEL)),
        'x_sample': nrm((DEC_BATCH, DEC_SEQ, D_MODEL)),
        'cache_k_a': nrm(pool + (N_KV_A, HEAD_DIM)),
        'cache_v_a': nrm(pool + (N_KV_A, HEAD_DIM)),
        'cache_k_idx': nrm(pool + (IDX_DIM,)),
        'cache_k_b': nrm(pool + (N_KV_B, 2, HEAD_DIM)),
        'cache_v_b': nrm(pool + (N_KV_B, 2 * HEAD_DIM)),
        'page_table': page_table,
        'c_prompt': nrm((BATCH, D_MODEL)),
        'c_sample': nrm((DEC_BATCH, D_MODEL)),
        'w_ada': nrm((L, D_MODEL, 6 * D_MODEL), 0.1 * fan(D_MODEL)),
        'b_ada': nrm((L, 6 * D_MODEL), 0.02),
        'w_in': nrm((L, D_MODEL, D_IN), fan(D_MODEL)),
        'w_br_a': nrm((L, OUT_A, D_MODEL), fan(OUT_A)),
        'w_br_b': nrm((L, OUT_B, D_MODEL), fan(OUT_B)),
        'w_o': nrm((L, D_MODEL, D_MODEL), DEEPNORM_BETA * fan(D_MODEL)),
        'lambda_q1': nrm((L, HEAD_DIM), 0.1),
        'lambda_k1': nrm((L, HEAD_DIM), 0.1),
        'lambda_q2': nrm((L, HEAD_DIM), 0.1),
        'lambda_k2': nrm((L, HEAD_DIM), 0.1),
        'subln_g': 1.0 + nrm((L, 2 * HEAD_DIM), 0.02),
        'ln1_g': 1.0 + nrm((L, D_MODEL), 0.02),
        'ln1_b': nrm((L, D_MODEL), 0.02),
        'ln2_g': 1.0 + nrm((L, D_MODEL), 0.02),
        'ln2_b': nrm((L, D_MODEL), 0.02),
        'w_router': nrm((L, D_MODEL, N_EXPERTS), fan(D_MODEL)),
        'b_router': nrm((L, N_EXPERTS), 0.01),
        'w_gate': nrm((L, N_EXPERTS, D_MODEL, D_FF), fan(D_MODEL)),
        'b_gate': nrm((L, N_EXPERTS, D_FF), 0.02),
        'w_up': nrm((L, N_EXPERTS, D_MODEL, D_FF), fan(D_MODEL)),
        'b_up': nrm((L, N_EXPERTS, D_FF), 0.02),
        'w_down': nrm((L, N_EXPERTS, D_FF, D_MODEL), DEEPNORM_BETA * fan(D_FF)),
        'b_down': nrm((L, N_EXPERTS, D_MODEL), 0.02),
    }


def reference(x_prompt, x_sample, cache_k_a, cache_v_a, cache_k_idx, cache_k_b, cache_v_b, page_table,
              c_prompt, c_sample, w_ada, b_ada, w_in, w_br_a, w_br_b, w_o,
              lambda_q1, lambda_k1, lambda_q2, lambda_k2, subln_g, ln1_g, ln1_b, ln2_g, ln2_b,
              w_router, b_router, w_gate, b_gate, w_up, b_up, w_down, b_down):
    pos_p = jnp.arange(SEQ)
    pos_s = PAST_LEN + jnp.arange(DEC_SEQ)
    yp, ys = x_prompt, x_sample
    st_p, st_s = [], []
    for l in range(DEPTH):
        w = dict(w_ada=w_ada[l], b_ada=b_ada[l], w_in=w_in[l], w_br_a=w_br_a[l], w_br_b=w_br_b[l], w_o=w_o[l],
                 ln1_g=ln1_g[l], ln1_b=ln1_b[l], ln2_g=ln2_g[l], ln2_b=ln2_b[l],
                 w_router=w_router[l], b_router=b_router[l], w_gate=w_gate[l], b_gate=b_gate[l],
                 w_up=w_up[l], b_up=b_up[l], w_down=w_down[l], b_down=b_down[l])
        lam_init = 0.8 - 0.6 * math.exp(-0.3 * l)
        lam = (jnp.exp(jnp.sum(lambda_q1[l].astype(jnp.float32) * lambda_k1[l].astype(jnp.float32)))
               - jnp.exp(jnp.sum(lambda_q2[l].astype(jnp.float32) * lambda_k2[l].astype(jnp.float32))) + lam_init)
        sub = subln_g[l]
        prompt_mix = lambda p: _prompt_mixers(p, lam, lam_init, sub)
        sample_mix = lambda p: _sample_mixers(p, cache_k_a[l], cache_v_a[l], cache_k_idx[l], cache_k_b[l],
                                              cache_v_b[l], page_table, lam, lam_init, sub)
        yp, rp = _layer(yp, c_prompt, pos_p, prompt_mix, w)
        ys, rs = _layer(ys, c_sample, pos_s, sample_mix, w)
        st_p.append(rp)
        st_s.append(rs)
    stk = lambda rows, name: jnp.stack([r[name] for r in rows])
    return (yp, ys,
            stk(st_p, 'ka'), stk(st_p, 'va'), stk(st_p, 'ki'), stk(st_p, 'kb'), stk(st_p, 'vb'),
            stk(st_s, 'ka'), stk(st_s, 'va'), stk(st_s, 'ki'), stk(st_s, 'kb'), stk(st_s, 'vb'))
```

```python
import functools
import math

import jax
import jax.numpy as jnp
import numpy as np
from jax import lax
from jax.experimental import pallas as pl
from jax.experimental.pallas import tpu as pltpu

F32, BF16, I32 = jnp.float32, jnp.bfloat16, jnp.int32

D_MODEL = 1024
HEAD_DIM = 64
N_HEADS_A, N_KV_A = 8, 2
N_IDX_HEADS, IDX_DIM = 8, 64
TOPK_MAX = 256
N_HEADS_B, N_KV_B = 4, 2
N_EXPERTS, TOP_K_EXPERTS, D_FF = 32, 4, 1024
SWIGLU_ALPHA, SWIGLU_LIMIT = 1.702, 7.0
ROPE_THETA = 10000.0
LN_EPS = 1e-5
PAGE = 128
Q_A, KV_A = N_HEADS_A * HEAD_DIM, N_KV_A * HEAD_DIM
Q_IDX = N_IDX_HEADS * IDX_DIM
Q_B, KV_B = N_HEADS_B * 2 * HEAD_DIM, N_KV_B * 2 * HEAD_DIM
PROJ_SIZES = (Q_A, KV_A, KV_A, Q_IDX, IDX_DIM, N_IDX_HEADS, Q_B, KV_B, KV_B, D_MODEL, D_MODEL)
IDX_SCALE = (N_IDX_HEADS * IDX_DIM) ** -0.5
QK_SCALE = HEAD_DIM ** -0.5

LANES = 128
UNIT = 16
TILE_UNITS = 16
TILE_M = UNIT * TILE_UNITS
INT_MIN = -(2 ** 31)
NEG_INF = float("-inf")
VMEM_LIMIT = 56 * 1024 * 1024

C_QA, C_QI, C_QB = 0, 512, 1024
C_KA, C_VA, C_KB, C_VB, C_KIWI, C_GA, C_GB = 1536, 1664, 1792, 2048, 2304, 2432, 3456
W_IN_COLS = 4480


def _params(sem=None):
    return pltpu.CompilerParams(dimension_semantics=sem, vmem_limit_bytes=VMEM_LIMIT)


def _ln(xf):
    mu = jnp.mean(xf, axis=-1, keepdims=True)
    xc = xf - mu
    var = jnp.mean(xc * xc, axis=-1, keepdims=True)
    return xc * lax.rsqrt(var + LN_EPS)


def _dot(a, b):
    return jnp.dot(a, b, preferred_element_type=F32)


def _dot_nt(a, b):
    return lax.dot_general(a, b, (((1,), (1,)), ((), ())), preferred_element_type=F32)


def _ada_kernel(c_ref, w_ref, b_ref, o_ref):
    c = c_ref[...]
    s = c * jax.nn.sigmoid(c)
    o_ref[...] = jnp.dot(s, w_ref[...], preferred_element_type=F32,
                         precision=lax.Precision.HIGHEST) + b_ref[...]


def _ada_call(c, w, b):
    n = c.shape[0]
    tn = 1024
    return pl.pallas_call(
        _ada_kernel,
        out_shape=jax.ShapeDtypeStruct((n, 6 * D_MODEL), F32),
        grid=(6 * D_MODEL // tn,),
        in_specs=[pl.BlockSpec((n, D_MODEL), lambda j: (0, 0)),
                  pl.BlockSpec((D_MODEL, tn), lambda j: (0, j)),
                  pl.BlockSpec((1, tn), lambda j: (0, j))],
        out_specs=pl.BlockSpec((n, tn), lambda j: (0, j)),
        compiler_params=_params(("arbitrary",)),
        name="ada",
    )(c, w, b.reshape(1, -1))


_PROJ_OUTS = (
    ("qa", 512, BF16), ("qi", 512, BF16), ("qb", 512, BF16),
    ("ka", 128, F32), ("va", 128, F32), ("kb", 256, F32), ("vb", 256, F32), ("ki", 64, F32),
    ("kab", 128, BF16), ("vab", 128, BF16), ("kbb", 256, BF16), ("vbb", 256, BF16), ("kib", 128, BF16),
    ("kiwi", 128, F32), ("sga", 1024, BF16), ("sgb", 1024, BF16),
)


def _proj_kernel(x_ref, sc_ref, sh_ref, w_ref, cos_ref, sin_ref, *outs):
    o = dict(zip([n for n, _, _ in _PROJ_OUTS], outs))
    tm = x_ref.shape[0]
    h = (_ln(x_ref[...]) * (1.0 + sc_ref[0]) + sh_ref[0]).astype(BF16)
    cos, sin = cos_ref[...], sin_ref[...]
    lane = lax.broadcasted_iota(I32, (tm, LANES), 1)
    lo_half = (lane % HEAD_DIM) < (HEAD_DIM // 2)

    def rope(v):
        sw = jnp.where(lo_half, pltpu.roll(v, LANES - HEAD_DIM // 2, 1), pltpu.roll(v, HEAD_DIM // 2, 1))
        return v * cos + sw * sin

    def proj(c0, width):
        return _dot(h, w_ref[:, c0:c0 + width])

    for name, c0 in (("qa", C_QA), ("qi", C_QI), ("qb", C_QB)):
        r = proj(c0, 512)
        for j in range(4):
            o[name][:, j * LANES:(j + 1) * LANES] = rope(r[:, j * LANES:(j + 1) * LANES]).astype(BF16)
    r = rope(proj(C_KA, 128))
    o["ka"][...] = r
    o["kab"][...] = r.astype(BF16)
    r = proj(C_VA, 128)
    o["va"][...] = r
    o["vab"][...] = r.astype(BF16)
    r = proj(C_KB, 256)
    for j in range(2):
        rr = rope(r[:, j * LANES:(j + 1) * LANES])
        o["kb"][:, j * LANES:(j + 1) * LANES] = rr
        o["kbb"][:, j * LANES:(j + 1) * LANES] = rr.astype(BF16)
    r = proj(C_VB, 256)
    o["vb"][...] = r
    o["vbb"][...] = r.astype(BF16)
    r = proj(C_KIWI, 128)
    rr = rope(r)
    o["ki"][...] = rr[:, :IDX_DIM]
    o["kib"][...] = rr.astype(BF16)
    o["kiwi"][...] = jnp.where(lane < IDX_DIM, rr, r * IDX_SCALE)
    o["sga"][...] = jax.nn.sigmoid(proj(C_GA, D_MODEL)).astype(BF16)
    o["sgb"][...] = jax.nn.sigmoid(proj(C_GB, D_MODEL)).astype(BF16)


def _proj_call(x, sc, sh, w, cos, sin, *, tm, mod_map, pos_map):
    n = x.shape[0]
    mod_rows = sc.shape[1]
    row = lambda i: (i, 0)
    return pl.pallas_call(
        _proj_kernel,
        out_shape=[jax.ShapeDtypeStruct((n, wd), dt) for _, wd, dt in _PROJ_OUTS],
        grid=(n // tm,),
        in_specs=[pl.BlockSpec((tm, D_MODEL), row),
                  pl.BlockSpec((1, mod_rows, D_MODEL), mod_map),
                  pl.BlockSpec((1, mod_rows, D_MODEL), mod_map),
                  pl.BlockSpec((D_MODEL, W_IN_COLS), lambda i: (0, 0)),
                  pl.BlockSpec((tm, LANES), pos_map),
                  pl.BlockSpec((tm, LANES), pos_map)],
        out_specs=[pl.BlockSpec((tm, wd), row) for _, wd, _ in _PROJ_OUTS],
        compiler_params=_params(("arbitrary",)),
        name="proj",
    )(x, sc, sh, w, cos, sin)


def _sortable_key(score):
    bits = pltpu.bitcast(score + 0.0, I32)
    return jnp.where(bits < 0, bits ^ jnp.int32(0x7FFFFFFF), bits)


def _kth_largest_key(count_ge, rows, k):
    t0 = jnp.where(count_ge(jnp.zeros((rows, 1), I32)) >= k, jnp.int32(0), jnp.int32(INT_MIN))

    def body(i, t):
        cand = t | jnp.left_shift(jnp.int32(1), 30 - i)
        return jnp.where(count_ge(cand) >= k, cand, t)

    return lax.fori_loop(0, 31, body, t0)


def _lambda_full(lam_ref, lam_init):
    l = lam_ref[...]
    a = jnp.exp(jnp.sum(l[0:1] * l[1:2], axis=-1, keepdims=True))
    b = jnp.exp(jnp.sum(l[2:3] * l[3:4], axis=-1, keepdims=True))
    return a - b + lam_init


def _attn_block(j, tq, topk, lam_init, qa_ref, qi_ref, qb_ref, kiwi_ref, kab_ref, vab_ref, kbb_ref, vbb_ref,
                kib_ref, lam_ref, sub_ref, oa_ref, ob_ref, key_ref, bias_ref):
    kw = (j + 1) * tq
    row = lax.broadcasted_iota(I32, (tq, kw), 0) + j * tq
    col = lax.broadcasted_iota(I32, (tq, kw), 1)
    adm = col <= row

    kiw = kiwi_ref[...]
    kib = kib_ref[0:kw, 0:IDX_DIM]
    score = jnp.zeros((tq, kw), F32)
    for h in range(N_IDX_HEADS):
        s = _dot_nt(qi_ref[:, h * IDX_DIM:(h + 1) * IDX_DIM], kib)
        score = score + kiw[:, IDX_DIM + h:IDX_DIM + h + 1] * jnp.maximum(s, 0.0)
    key_ref[:, 0:kw] = jnp.where(adm, _sortable_key(score), jnp.int32(INT_MIN))

    def count_ge(t):
        return jnp.sum(jnp.where(key_ref[:, 0:kw] >= t, 1.0, 0.0), axis=-1, keepdims=True)

    thr = _kth_largest_key(count_ge, tq, float(topk))
    key = key_ref[:, 0:kw]
    n_ge = count_ge(thr)
    bias_ref[:, 0:kw] = jnp.where((key >= thr) & adm, 0.0, NEG_INF)
    tie = jnp.max(jnp.where((thr > INT_MIN) & (n_ge > float(topk)), 1.0, 0.0))

    @pl.when(tie > 0.0)
    def _():
        k2 = key_ref[:, 0:kw]
        gt = k2 > thr
        eq = k2 == thr
        need = float(topk) - jnp.sum(jnp.where(gt, 1.0, 0.0), axis=-1, keepdims=True)
        before = (lax.broadcasted_iota(I32, (kw, kw), 0) < lax.broadcasted_iota(I32, (kw, kw), 1))
        prefix = _dot(jnp.where(eq, 1.0, 0.0).astype(BF16), jnp.where(before, 1.0, 0.0).astype(BF16))
        sel = (gt | (eq & (prefix < need))) & (col <= row)
        bias_ref[:, 0:kw] = jnp.where(sel, 0.0, NEG_INF)

    bias = bias_ref[:, 0:kw]
    for g in range(N_KV_A):
        kg = kab_ref[0:kw, g * HEAD_DIM:(g + 1) * HEAD_DIM]
        vg = vab_ref[0:kw, g * HEAD_DIM:(g + 1) * HEAD_DIM]
        heads = []
        for r in range(N_HEADS_A // N_KV_A):
            hh = g * (N_HEADS_A // N_KV_A) + r
            s = _dot_nt(qa_ref[:, hh * HEAD_DIM:(hh + 1) * HEAD_DIM], kg) + bias
            p = jnp.exp(s - jnp.max(s, axis=-1, keepdims=True))
            l = jnp.sum(p, axis=-1, keepdims=True)
            heads.append(_dot(p.astype(BF16), vg) / l)
        c0 = g * (N_HEADS_A // N_KV_A) * HEAD_DIM
        oa_ref[:, c0:c0 + len(heads) * HEAD_DIM] = jnp.concatenate(heads, axis=-1).astype(BF16)

    lam = _lambda_full(lam_ref, lam_init)
    cbias = jnp.where(adm, 0.0, NEG_INF)
    rb = N_HEADS_B // N_KV_B
    for g in range(N_KV_B):
        vg = vbb_ref[0:kw, g * 2 * HEAD_DIM:(g + 1) * 2 * HEAD_DIM]
        for r in range(rb):
            ps = []
            for c in range(2):
                qc0 = ((g * rb + r) * 2 + c) * HEAD_DIM
                kc0 = (g * 2 + c) * HEAD_DIM
                s = _dot_nt(qb_ref[:, qc0:qc0 + HEAD_DIM], kbb_ref[0:kw, kc0:kc0 + HEAD_DIM]) + cbias
                p = jnp.exp(s - jnp.max(s, axis=-1, keepdims=True))
                ps.append(p / jnp.sum(p, axis=-1, keepdims=True))
            a = ps[0] - lam * ps[1]
            o = _dot(a.astype(BF16), vg)
            o = o * lax.rsqrt(jnp.mean(o * o, axis=-1, keepdims=True) + LN_EPS) * sub_ref[...] * (1.0 - lam_init)
            oc0 = (g * rb + r) * 2 * HEAD_DIM
            ob_ref[:, oc0:oc0 + 2 * HEAD_DIM] = o.astype(BF16)


def _attn_kernel(tq, nq, topk, lam_init, *refs):
    i = pl.program_id(1)
    for j in range(nq):
        @pl.when(i == j)
        def _(j=j):
            _attn_block(j, tq, topk, lam_init, *refs)


def _attn_call(p, lam4, subln, n_batch, seq, lam_init, *, tq):
    nq = seq // tq
    topk = min(TOPK_MAX, seq // 4)
    n = n_batch * seq
    qmap = lambda b, i: (b * nq + i, 0)
    kmap = lambda b, i: (b, 0)
    const = lambda b, i: (0, 0)
    return pl.pallas_call(
        functools.partial(_attn_kernel, tq, nq, topk, lam_init),
        out_shape=[jax.ShapeDtypeStruct((n, Q_A), BF16), jax.ShapeDtypeStruct((n, Q_B), BF16)],
        grid=(n_batch, nq),
        in_specs=[pl.BlockSpec((tq, 512), qmap), pl.BlockSpec((tq, 512), qmap), pl.BlockSpec((tq, 512), qmap),
                  pl.BlockSpec((tq, LANES), qmap),
                  pl.BlockSpec((seq, 128), kmap), pl.BlockSpec((seq, 128), kmap),
                  pl.BlockSpec((seq, 256), kmap), pl.BlockSpec((seq, 256), kmap),
                  pl.BlockSpec((seq, 128), kmap),
                  pl.BlockSpec((4, HEAD_DIM), const), pl.BlockSpec((1, 2 * HEAD_DIM), const)],
        out_specs=[pl.BlockSpec((tq, Q_A), qmap), pl.BlockSpec((tq, Q_B), qmap)],
        scratch_shapes=[pltpu.VMEM((tq, seq), I32), pltpu.VMEM((tq, seq), F32)],
        compiler_params=_params(("arbitrary", "arbitrary")),
        name="attn",
    )(p["qa"], p["qi"], p["qb"], p["kiwi"], p["kab"], p["vab"], p["kbb"], p["vbb"], p["kib"], lam4, subln)


def _sample_kernel(n_chunks, ppc, topk, lam_init,
                   pt_ref, qi_ref, qa_ref, qb_ref, wi_ref, kin_ref, kan_ref, van_ref, kbn_ref, vbn_ref,
                   lam_ref, sub_ref, cki_ref, cka_ref, cva_ref, ckb_ref, cvb_ref,
                   oa_ref, ob_ref,
                   b64, b128, b256, sem, key_ref, bias_ref, s_ref, acc_ref, l_ref):
    b = pl.program_id(0)
    ch = ppc * PAGE
    past = n_chunks * ch
    width = past + LANES
    nq = 8

    def stream(cache, buf, fn):
        def copy(c, slot, pg):
            return pltpu.make_async_copy(cache.at[pt_ref[b, c * ppc + pg]],
                                         buf.at[slot, pl.ds(pg * PAGE, PAGE)], sem.at[slot])

        def start(c, slot):
            for pg in range(ppc):
                copy(c, slot, pg).start()

        start(0, 0)

        def body(c, carry):
            slot = c % 2

            @pl.when(c + 1 < n_chunks)
            def _():
                start(c + 1, 1 - slot)

            for pg in range(ppc):
                copy(c, slot, pg).wait()
            fn(c, buf.at[slot])
            return carry

        lax.fori_loop(0, n_chunks, body, 0)

    def chunk_ds(c):
        return pl.ds(pl.multiple_of(c * ch, ch), ch)

    newcol = lax.broadcasted_iota(I32, (nq, LANES), 1)
    newrow = lax.broadcasted_iota(I32, (nq, LANES), 0)
    new_adm = newcol <= newrow

    wi = wi_ref[0]

    def head_sum(s):
        acc = jnp.zeros((nq, s.shape[1]), F32)
        for h in range(N_IDX_HEADS):
            acc = acc + wi[:, IDX_DIM + h:IDX_DIM + h + 1] * jnp.maximum(s[h * nq:(h + 1) * nq], 0.0)
        return acc

    def idx_fn(c, kbuf):
        s = _dot_nt(qi_ref[0], kbuf[...].astype(BF16))
        key_ref[:, chunk_ds(c)] = _sortable_key(head_sum(s))

    stream(cki_ref, b64, idx_fn)
    s_new = head_sum(_dot_nt(qi_ref[0], kin_ref[0]))
    key_ref[:, past:width] = jnp.where(new_adm, _sortable_key(s_new), jnp.int32(INT_MIN))

    def count_ge(t):
        return jnp.sum(jnp.where(key_ref[...] >= t, 1.0, 0.0), axis=-1, keepdims=True)

    thr = _kth_largest_key(count_ge, nq, float(topk))
    key = key_ref[...]
    n_ge = count_ge(thr)
    bias_ref[...] = jnp.where((key >= thr) & (key > INT_MIN), 0.0, NEG_INF)
    tie = jnp.max(jnp.where((thr > INT_MIN) & (n_ge > float(topk)), 1.0, 0.0))

    @pl.when(tie > 0.0)
    def _():
        need = float(topk) - jnp.sum(jnp.where(key_ref[...] > thr, 1.0, 0.0), axis=-1, keepdims=True)

        def fix(lo, n, seen):
            k2 = key_ref[:, pl.ds(lo, n)]
            eq = k2 == thr
            before = lax.broadcasted_iota(I32, (n, n), 0) < lax.broadcasted_iota(I32, (n, n), 1)
            eqf = jnp.where(eq, 1.0, 0.0)
            prefix = _dot(eqf.astype(BF16), jnp.where(before, 1.0, 0.0).astype(BF16)) + seen
            sel = ((k2 > thr) | (eq & (prefix < need))) & (k2 > INT_MIN)
            bias_ref[:, pl.ds(lo, n)] = jnp.where(sel, 0.0, NEG_INF)
            return seen + jnp.sum(eqf, axis=-1, keepdims=True)

        seen = lax.fori_loop(0, n_chunks, lambda c, sn: fix(pl.multiple_of(c * ch, ch), ch, sn),
                             jnp.zeros((nq, 1), F32))
        fix(past, LANES, seen)

    def attend(q_ref, ck_ref, kbuf, k_new_ref, cv_ref, vbuf, v_new_ref, bias_of):
        groups = q_ref.shape[1] // nq

        def tile_rows(x):
            return jnp.concatenate([x] * groups, axis=0)

        def k_fn(c, kb):
            s_ref[:, chunk_ds(c)] = _dot_nt(q_ref[0], kb[...].astype(BF16)) + tile_rows(bias_of(chunk_ds(c), ch))

        stream(ck_ref, kbuf, k_fn)
        s_ref[:, past:width] = _dot_nt(q_ref[0], k_new_ref[0]) + tile_rows(bias_of(pl.ds(past, LANES), LANES))
        m = jnp.max(s_ref[...], axis=-1, keepdims=True)
        vw = vbuf.shape[-1]
        acc_ref[...] = jnp.zeros_like(acc_ref)
        l_ref[...] = jnp.zeros_like(l_ref)

        def add(p, v):
            acc_ref[:, 0:vw] += _dot(p.astype(BF16), v)
            l_ref[...] += jnp.sum(p, axis=-1, keepdims=True)

        def v_fn(c, vb):
            add(jnp.exp(s_ref[:, chunk_ds(c)] - m), vb[...].astype(BF16))

        stream(cv_ref, vbuf, v_fn)
        add(jnp.exp(s_ref[:, past:width] - m), v_new_ref[0])
        return acc_ref[:, 0:vw] / l_ref[:, 0:1]

    o = attend(qa_ref, cka_ref, b128, kan_ref, cva_ref, b128, van_ref, lambda ds, n: bias_ref[:, ds])
    ra = N_HEADS_A // N_KV_A
    for g in range(N_KV_A):
        for r in range(ra):
            hh = g * ra + r
            oa_ref[0, :, hh * HEAD_DIM:(hh + 1) * HEAD_DIM] = o[hh * nq:(hh + 1) * nq, g * HEAD_DIM:(g + 1) * HEAD_DIM]

    cb_new = jnp.where(new_adm, 0.0, NEG_INF)

    def bias_b(ds, n):
        return cb_new if n == LANES else jnp.zeros((nq, n), F32)

    o = attend(qb_ref, ckb_ref, b256, kbn_ref, cvb_ref, b256, vbn_ref, bias_b)
    lam = _lambda_full(lam_ref, lam_init)
    rb = N_HEADS_B // N_KV_B
    for g in range(N_KV_B):
        for r in range(rb):
            base = (g * rb + r) * 2 * nq
            cols = slice(g * 2 * HEAD_DIM, (g + 1) * 2 * HEAD_DIM)
            d = o[base:base + nq, cols] - lam * o[base + nq:base + 2 * nq, cols]
            d = d * lax.rsqrt(jnp.mean(d * d, axis=-1, keepdims=True) + LN_EPS) * sub_ref[...] * (1.0 - lam_init)
            oc0 = (g * rb + r) * 2 * HEAD_DIM
            ob_ref[0, :, oc0:oc0 + 2 * HEAD_DIM] = d


def _sample_call(page_table, qi, qa, qb, wi, kin, kan, van, kbn, vbn, lam4, subln,
                 cki, cka, cva, ckb, cvb, lam_init, *, ppc):
    nb, n_pages = page_table.shape
    n_chunks = n_pages // ppc
    ch = ppc * PAGE
    past = n_pages * PAGE
    width = past + LANES
    topk = min(TOPK_MAX, (past + 8) // 4)
    b3 = lambda b, pt: (b, 0, 0)
    const = lambda b, pt: (0, 0)
    any_spec = pl.BlockSpec(memory_space=pl.ANY)
    full = lambda a: pl.BlockSpec((1,) + a.shape[1:], b3)
    grid_spec = pltpu.PrefetchScalarGridSpec(
        num_scalar_prefetch=1,
        grid=(nb,),
        in_specs=[full(qi), full(qa), full(qb), full(wi), full(kin), full(kan), full(van), full(kbn), full(vbn),
                  pl.BlockSpec((4, HEAD_DIM), const), pl.BlockSpec((1, 2 * HEAD_DIM), const),
                  any_spec, any_spec, any_spec, any_spec, any_spec],
        out_specs=[pl.BlockSpec((1, 8, Q_A), b3), pl.BlockSpec((1, 8, Q_B), b3)],
        scratch_shapes=[pltpu.VMEM((2, ch, 64), F32), pltpu.VMEM((2, ch, 128), F32), pltpu.VMEM((2, ch, 256), F32),
                        pltpu.SemaphoreType.DMA((2,)),
                        pltpu.VMEM((8, width), I32), pltpu.VMEM((8, width), F32),
                        pltpu.VMEM((64, width), F32), pltpu.VMEM((64, 256), F32), pltpu.VMEM((64, LANES), F32)],
    )
    return pl.pallas_call(
        functools.partial(_sample_kernel, n_chunks, ppc, topk, lam_init),
        out_shape=[jax.ShapeDtypeStruct((nb, 8, Q_A), F32), jax.ShapeDtypeStruct((nb, 8, Q_B), F32)],
        grid_spec=grid_spec,
        compiler_params=_params(("arbitrary",)),
        name="sample_mix",
    )(page_table, qi, qa, qb, wi, kin, kan, van, kbn, vbn, lam4, subln, cki, cka, cva, ckb, cvb)


def _split_bf16(x):
    hi = x.astype(BF16)
    return hi, (x - hi.astype(F32)).astype(BF16)


def _outproj_kernel(alpha, oa_ref, ob_ref, sga_ref, sgb_ref, x_ref, g1_ref, sc2_ref, sh2_ref,
                    wa_ref, wb_ref, wo_ref, lg_ref, lb_ref, wrh_ref, wrl_ref, br_ref,
                    x1_ref, h2_ref, lo_ref):
    merged = (sga_ref[...].astype(F32) * _dot(oa_ref[...], wa_ref[...])
              + sgb_ref[...].astype(F32) * _dot(ob_ref[...], wb_ref[...]))
    y = _dot(merged.astype(BF16), wo_ref[...])
    x1 = _ln(alpha * x_ref[...] + (1.0 + g1_ref[0]) * y) * lg_ref[...] + lb_ref[...]
    x1_ref[...] = x1
    h2 = _ln(x1) * (1.0 + sc2_ref[0]) + sh2_ref[0]
    h2_ref[...] = h2.astype(BF16)
    hi, lo = _split_bf16(h2)
    logits = _dot(hi, wrh_ref[...]) + (_dot(lo, wrh_ref[...]) + _dot(hi, wrl_ref[...])) + br_ref[...]
    lane = lax.broadcasted_iota(I32, logits.shape, 1)
    lo_ref[...] = jnp.where(lane < N_EXPERTS, logits, NEG_INF)


def _outproj_call(oa, ob, sga, sgb, x, g1, sc2, sh2, wa, wb, wo, lg, lb, wrh, wrl, br, alpha, *, tm, mod_map):
    n = x.shape[0]
    mod_rows = g1.shape[1]
    row = lambda i: (i, 0)
    const = lambda i: (0, 0)
    mod = pl.BlockSpec((1, mod_rows, D_MODEL), mod_map)
    return pl.pallas_call(
        functools.partial(_outproj_kernel, alpha),
        out_shape=[jax.ShapeDtypeStruct((n, D_MODEL), F32), jax.ShapeDtypeStruct((n, D_MODEL), BF16),
                   jax.ShapeDtypeStruct((n, LANES), F32)],
        grid=(n // tm,),
        in_specs=[pl.BlockSpec((tm, 512), row), pl.BlockSpec((tm, 512), row),
                  pl.BlockSpec((tm, D_MODEL), row), pl.BlockSpec((tm, D_MODEL), row),
                  pl.BlockSpec((tm, D_MODEL), row), mod, mod, mod,
                  pl.BlockSpec((512, D_MODEL), const), pl.BlockSpec((512, D_MODEL), const),
                  pl.BlockSpec((D_MODEL, D_MODEL), const),
                  pl.BlockSpec((1, D_MODEL), const), pl.BlockSpec((1, D_MODEL), const),
                  pl.BlockSpec((D_MODEL, LANES), const), pl.BlockSpec((D_MODEL, LANES), const),
                  pl.BlockSpec((1, LANES), const)],
        out_specs=[pl.BlockSpec((tm, D_MODEL), row), pl.BlockSpec((tm, D_MODEL), row), pl.BlockSpec((tm, LANES), row)],
        compiler_params=_params(("arbitrary",)),
        name="outproj",
    )(oa, ob, sga, sgb, x, g1, sc2, sh2, wa, wb, wo, lg, lb, wrh, wrl, br)


def _dispatch_kernel(slots, n_real, lo_ref, h2_ref, *rest):
    xs_ref, route_ref, cnt_ref = rest[-3:]

    @pl.when(pl.program_id(0) < n_real)
    def _():
        _dispatch_block(slots, lo_ref, h2_ref, xs_ref, route_ref, cnt_ref)

    @pl.when(pl.program_id(0) >= n_real)
    def _():
        xs_ref[...] = jnp.zeros_like(xs_ref)


def _dispatch_block(slots, lo_ref, h2_ref, xs_ref, route_ref, cnt_ref):
    tb = lo_ref.shape[0]
    lane = lax.broadcasted_iota(I32, (tb, LANES), 1)
    l = lo_ref[...]
    vals, hots = [], []
    for _ in range(TOP_K_EXPERTS):
        m = jnp.max(l, axis=-1, keepdims=True)
        idx = jnp.min(jnp.where(l == m, lane, LANES), axis=-1, keepdims=True)
        hot = lane == idx
        vals.append(m)
        hots.append(hot)
        l = jnp.where(hot, NEG_INF, l)
    es = [jnp.exp(v - vals[0]) for v in vals]
    den = es[0] + es[1] + es[2] + es[3]
    ws = [e / den for e in es]
    sel = jnp.zeros((tb, LANES), F32)
    for hot in hots:
        sel = sel + jnp.where(hot, 1.0, 0.0)
    earlier = lax.broadcasted_iota(I32, (tb, tb), 1) < lax.broadcasted_iota(I32, (tb, tb), 0)
    pos = _dot(jnp.where(earlier, 1.0, 0.0).astype(BF16), sel.astype(BF16))
    cnt = jnp.sum(sel, axis=0, keepdims=True)
    run_units = jnp.right_shift(cnt.astype(I32) + (UNIT - 1), 4).astype(F32)
    lower = lax.broadcasted_iota(I32, (LANES, LANES), 0) < lax.broadcasted_iota(I32, (LANES, LANES), 1)
    off_units = _dot(jnp.broadcast_to(run_units, (8, LANES)).astype(BF16), jnp.where(lower, 1.0, 0.0).astype(BF16))
    dest = off_units[0:1] * float(UNIT) + pos
    route = jnp.zeros((tb, LANES), F32)
    for j in range(TOP_K_EXPERTS):
        dj = jnp.sum(jnp.where(hots[j], dest, 0.0), axis=-1, keepdims=True)
        route = route + jnp.where(lane == j, dj, 0.0) + jnp.where(lane == TOP_K_EXPERTS + j, ws[j], 0.0)
    route_ref[...] = route
    cnt_ref[...] = jnp.broadcast_to(cnt, (8, LANES))
    route_t = route.T
    slot = lax.broadcasted_iota(I32, (slots, tb), 0).astype(F32)
    hit = slot == route_t[0:1]
    for j in range(1, TOP_K_EXPERTS):
        hit = hit | (slot == route_t[j:j + 1])
    xs_ref[...] = _dot(jnp.where(hit, 1.0, 0.0).astype(BF16), h2_ref[...]).astype(BF16)


def _dispatch_call(logits, h2, *, tb, slots, total_blocks, blk0=0, xs_buf=None):
    n = h2.shape[0]
    nb = n // tb
    steps = nb if xs_buf is not None else total_blocks - blk0
    row = lambda i: (jnp.minimum(i, nb - 1), 0)
    args, in_specs, aliases = [logits, h2], [pl.BlockSpec((tb, LANES), row), pl.BlockSpec((tb, D_MODEL), row)], {}
    if xs_buf is not None:
        args.append(xs_buf)
        in_specs.append(pl.BlockSpec(memory_space=pl.ANY))
        aliases = {2: 0}
    return pl.pallas_call(
        functools.partial(_dispatch_kernel, slots, nb),
        out_shape=[jax.ShapeDtypeStruct((total_blocks * slots, D_MODEL), BF16), jax.ShapeDtypeStruct((n, LANES), F32),
                   jax.ShapeDtypeStruct((nb * 8, LANES), F32)],
        grid=(steps,),
        in_specs=in_specs,
        out_specs=[pl.BlockSpec((slots, D_MODEL), lambda i: (blk0 + i, 0)), pl.BlockSpec((tb, LANES), row),
                   pl.BlockSpec((8, LANES), row)],
        input_output_aliases=aliases,
        compiler_params=_params(("arbitrary",)),
        name="dispatch",
    )(*args)


def _ffn_kernel(te_ref, src_ref, nt_ref, xs_ref, wg_ref, bg_ref, wu_ref, bu_ref, wd_ref, bd_ref, ys_ref,
                xbuf, ybuf, insem, outsem):
    j = pl.program_id(0)
    nt = nt_ref[0]
    slot = j % 2

    def in_copy(tile, s, k):
        u = jnp.maximum(src_ref[tile * TILE_UNITS + k], 0)
        return pltpu.make_async_copy(xs_ref.at[u], xbuf.at[s, pl.ds(k * UNIT, UNIT)], insem.at[s])

    def out_copy(tile, s, k):
        u = jnp.maximum(src_ref[tile * TILE_UNITS + k], 0)
        return pltpu.make_async_copy(ybuf.at[s, pl.ds(k * UNIT, UNIT)], ys_ref.at[u], outsem.at[s])

    def for_valid_units(tile, fn):
        for k in range(TILE_UNITS):
            @pl.when(src_ref[tile * TILE_UNITS + k] >= 0)
            def _(k=k):
                fn(k)

    def start_in(tile, s):
        for_valid_units(tile, lambda k: in_copy(tile, s, k).start())

    def wait_out(tile, s):
        for_valid_units(tile, lambda k: out_copy(tile, s, k).wait())

    @pl.when(j == 0)
    def _():
        xbuf[...] = jnp.zeros_like(xbuf)

        @pl.when(nt > 0)
        def _():
            start_in(0, 0)

    @pl.when(j + 1 < nt)
    def _():
        start_in(j + 1, 1 - slot)

    @pl.when(j < nt)
    def _():
        for_valid_units(j, lambda k: in_copy(j, slot, k).wait())
        x = xbuf[slot]
        gate = jnp.minimum(_dot(x, wg_ref[0]) + bg_ref[0], SWIGLU_LIMIT)
        up = jnp.clip(_dot(x, wu_ref[0]) + bu_ref[0], -SWIGLU_LIMIT, SWIGLU_LIMIT)
        act = (up + 1.0) * gate * jax.nn.sigmoid(SWIGLU_ALPHA * gate)
        y = _dot(act.astype(BF16), wd_ref[0]) + bd_ref[0]

        @pl.when(j >= 2)
        def _():
            wait_out(j - 2, slot)

        ybuf[slot] = y.astype(BF16)
        for_valid_units(j, lambda k: out_copy(j, slot, k).start())

    @pl.when(j == pl.num_programs(0) - 1)
    def _():
        @pl.when(nt >= 2)
        def _():
            wait_out(nt - 2, nt % 2)

        @pl.when(nt >= 1)
        def _():
            wait_out(nt - 1, (nt - 1) % 2)


def _ffn_call(tile_expert, src_units, n_tiles, xs_units, wg, bg, wu, bu, wd, bd):
    n_units = xs_units.shape[0]
    max_tiles = tile_expert.shape[0]
    wmap = lambda j, te, src, nt: (te[j], 0, 0)
    any_spec = pl.BlockSpec(memory_space=pl.ANY)
    grid_spec = pltpu.PrefetchScalarGridSpec(
        num_scalar_prefetch=3,
        grid=(max_tiles,),
        in_specs=[any_spec,
                  pl.BlockSpec((1, D_MODEL, D_FF), wmap), pl.BlockSpec((1, 1, D_FF), wmap),
                  pl.BlockSpec((1, D_MODEL, D_FF), wmap), pl.BlockSpec((1, 1, D_FF), wmap),
                  pl.BlockSpec((1, D_FF, D_MODEL), wmap), pl.BlockSpec((1, 1, D_MODEL), wmap)],
        out_specs=any_spec,
        scratch_shapes=[pltpu.VMEM((2, TILE_M, D_MODEL), BF16), pltpu.VMEM((2, TILE_M, D_MODEL), BF16),
                        pltpu.SemaphoreType.DMA((2,)), pltpu.SemaphoreType.DMA((2,))],
    )
    return pl.pallas_call(
        _ffn_kernel,
        out_shape=jax.ShapeDtypeStruct((n_units, UNIT, D_MODEL), BF16),
        grid_spec=grid_spec,
        input_output_aliases={3: 0},
        compiler_params=_params(("arbitrary",)),
        name="ffn",
    )(tile_expert, src_units, n_tiles, xs_units, wg, bg, wu, bu, wd, bd)


def _unit_tables(cnt, units_per_block, max_tiles):
    nb = cnt.shape[0]
    run_units = (cnt + (UNIT - 1)) // UNIT
    blk_off = jnp.cumsum(run_units, axis=1) - run_units
    cum = jnp.cumsum(run_units, axis=0)
    tot = cum[-1]
    tiles = (tot + (TILE_UNITS - 1)) // TILE_UNITS
    tile_end = jnp.cumsum(tiles)
    n_tiles = tile_end[-1]
    j = jnp.arange(max_tiles, dtype=I32)
    te = jnp.minimum(jnp.searchsorted(tile_end, j, side="right"), N_EXPERTS - 1).astype(I32)
    last_e = jnp.minimum(jnp.searchsorted(tile_end, n_tiles - 1, side="right"), N_EXPERTS - 1).astype(I32)
    te = jnp.where(j < n_tiles, te, last_e)
    tile_start = (tile_end - tiles)[te]
    rank = ((j - tile_start) * TILE_UNITS)[:, None] + jnp.arange(TILE_UNITS, dtype=I32)[None, :]
    cum_t = cum.T[te]
    blk = jnp.sum(cum_t[:, None, :] <= rank[:, :, None], axis=-1).astype(I32)
    blk_c = jnp.minimum(blk, nb - 1)
    before = jnp.take_along_axis(cum_t - run_units.T[te], blk_c, axis=1)
    unit = blk_c * units_per_block + blk_off[blk_c, te[:, None]] + (rank - before)
    valid = (rank < tot[te][:, None]) & (j < n_tiles)[:, None]
    src = jnp.where(valid, unit, -1).astype(I32).reshape(-1)
    return te, src, n_tiles.astype(I32).reshape(1)


def _combine_kernel(alpha, ys_ref, route_ref, x1_ref, g2_ref, lg_ref, lb_ref, y_ref):
    tb = x1_ref.shape[0]
    slots = ys_ref.shape[0]
    route = route_ref[...]
    slot = lax.broadcasted_iota(I32, (tb, slots), 1).astype(F32)
    pw = jnp.zeros((tb, slots), F32)
    for j in range(TOP_K_EXPERTS):
        pw = pw + jnp.where(slot == route[:, j:j + 1], route[:, TOP_K_EXPERTS + j:TOP_K_EXPERTS + j + 1], 0.0)
    moe = _dot(pw.astype(BF16), ys_ref[...])
    y_ref[...] = _ln(alpha * x1_ref[...] + (1.0 + g2_ref[0]) * moe) * lg_ref[...] + lb_ref[...]


def _combine_call(ys, route, x1, g2, lg, lb, alpha, *, tb, slots, blk0, mod_map):
    n = x1.shape[0]
    mod_rows = g2.shape[1]
    row = lambda i: (i, 0)
    const = lambda i: (0, 0)
    return pl.pallas_call(
        functools.partial(_combine_kernel, alpha),
        out_shape=jax.ShapeDtypeStruct((n, D_MODEL), F32),
        grid=(n // tb,),
        in_specs=[pl.BlockSpec((slots, D_MODEL), lambda i: (blk0 + i, 0)),
                  pl.BlockSpec((tb, LANES), row), pl.BlockSpec((tb, D_MODEL), row),
                  pl.BlockSpec((1, mod_rows, D_MODEL), mod_map),
                  pl.BlockSpec((1, D_MODEL), const), pl.BlockSpec((1, D_MODEL), const)],
        out_specs=pl.BlockSpec((tb, D_MODEL), row),
        compiler_params=_params(("arbitrary",)),
        name="combine",
    )(ys, route, x1, g2, lg, lb)


def _rope_tables(pos):
    half = HEAD_DIM // 2
    inv = ROPE_THETA ** (-np.arange(half, dtype=np.float64) / half)
    ang = np.asarray(pos, np.float64)[:, None] * inv[None, :]
    cos = np.tile(np.cos(ang), (1, LANES // half))
    sin = np.tile(np.concatenate([-np.sin(ang), np.sin(ang)], axis=1), (1, LANES // HEAD_DIM))
    return jnp.asarray(cos, F32), jnp.asarray(sin, F32)


def _prep_w_in(w):
    offs = np.concatenate([[0], np.cumsum(PROJ_SIZES)])
    qa, ka, va, qi, ki, wi, qb, kb, vb, ga, gb = [w[:, offs[i]:offs[i + 1]] for i in range(11)]
    pad = jnp.zeros((D_MODEL, LANES - IDX_DIM - N_IDX_HEADS), w.dtype)
    return jnp.concatenate([qa * QK_SCALE, qi, qb * QK_SCALE, ka, va, kb, vb, ki, wi, pad, ga, gb], axis=1).astype(BF16)


def _block_diag_rows(q, n_blocks):
    eye = jnp.eye(n_blocks, dtype=q.dtype)
    nb, _, rows, d = q.shape
    return jnp.einsum("bgrd,gh->bgrhd", q, eye).reshape(nb, n_blocks * rows, n_blocks * d)


def kernel(x_prompt, x_sample, cache_k_a, cache_v_a, cache_k_idx, cache_k_b, cache_v_b, page_table, c_prompt, c_sample, w_ada, b_ada, w_in, w_br_a, w_br_b, w_o, lambda_q1, lambda_k1, lambda_q2, lambda_k2, subln_g, ln1_g, ln1_b, ln2_g, ln2_b, w_router, b_router, w_gate, b_gate, w_up, b_up, w_down, b_down):
    nbp, seq, _ = x_prompt.shape
    nbs, dseq, _ = x_sample.shape
    depth = w_ada.shape[0]
    assert depth == 1 and dseq == 8
    n_pages = page_table.shape[1]
    past = n_pages * PAGE
    alpha = (2 * depth) ** 0.25
    lam_init = 0.8 - 0.6 * math.exp(-0.3 * 0)
    np_tok, ns_tok = nbp * seq, nbs * dseq

    tm = min(512, seq)
    tq = min(256, seq)
    tb = tm
    slots = ((4 * tb + N_EXPERTS * (UNIT - 1) + TILE_M - 1) // TILE_M) * TILE_M
    upb = slots // UNIT
    ppc = min(16, n_pages)

    w_in_r = _prep_w_in(w_in[0])
    wa, wb, wo = w_br_a[0].astype(BF16), w_br_b[0].astype(BF16), w_o[0].astype(BF16)
    wr = jnp.pad(w_router[0], ((0, 0), (0, LANES - N_EXPERTS)))
    wrh = wr.astype(BF16)
    wrl = (wr - wrh.astype(F32)).astype(BF16)
    br = jnp.pad(b_router[0], (0, LANES - N_EXPERTS)).reshape(1, LANES)
    wg, wu, wd = w_gate[0].astype(BF16), w_up[0].astype(BF16), w_down[0].astype(BF16)
    bg, bu, bd = b_gate[0][:, None, :], b_up[0][:, None, :], b_down[0][:, None, :]
    lam4 = jnp.stack([lambda_q1[0], lambda_k1[0], lambda_q2[0], lambda_k2[0]])
    subln = subln_g[0].reshape(1, -1)
    l1g, l1b, l2g, l2b = (a[0].reshape(1, -1) for a in (ln1_g, ln1_b, ln2_g, ln2_b))

    mod = _ada_call(jnp.concatenate([c_prompt, c_sample], axis=0), w_ada[0], b_ada[0])
    mod_p = [m.reshape(nbp, 1, D_MODEL) for m in jnp.split(mod[:nbp], 6, axis=-1)]
    mod_s = [jnp.repeat(m, dseq, axis=0).reshape(1, ns_tok, D_MODEL) for m in jnp.split(mod[nbp:], 6, axis=-1)]
    tiles_per_seq = seq // tm
    pmap = lambda i: (i // tiles_per_seq, 0, 0)
    smap = lambda i: (0, 0, 0)

    cos_p, sin_p = _rope_tables(np.arange(seq))
    cos_s, sin_s = _rope_tables(np.tile(past + np.arange(dseq), nbs))
    xp = x_prompt.reshape(np_tok, D_MODEL)
    xs = x_sample.reshape(ns_tok, D_MODEL)
    names = [n for n, _, _ in _PROJ_OUTS]
    pp = dict(zip(names, _proj_call(xp, mod_p[1], mod_p[0], w_in_r, cos_p, sin_p, tm=tm, mod_map=pmap,
                                    pos_map=lambda i: (i % tiles_per_seq, 0))))
    ps = dict(zip(names, _proj_call(xs, mod_s[1], mod_s[0], w_in_r, cos_s, sin_s, tm=ns_tok, mod_map=smap,
                                    pos_map=lambda i: (0, 0))))

    oa_p, ob_p = _attn_call(pp, lam4, subln, nbp, seq, lam_init, tq=tq)

    def rows(q, n_heads):
        return q.reshape(nbs, dseq, n_heads, HEAD_DIM).transpose(0, 2, 1, 3)

    qi_s = rows(ps["qi"], N_IDX_HEADS).reshape(nbs, N_IDX_HEADS * dseq, IDX_DIM)
    qa_s = _block_diag_rows(rows(ps["qa"], N_HEADS_A).reshape(nbs, N_KV_A, (N_HEADS_A // N_KV_A) * dseq, HEAD_DIM), N_KV_A)
    qb_r = rows(ps["qb"], 2 * N_HEADS_B)
    rb = N_HEADS_B // N_KV_B
    qb_r = qb_r.reshape(nbs, N_KV_B, rb, 2, dseq, HEAD_DIM)
    eye2 = jnp.eye(2, dtype=BF16)
    qb_s = jnp.einsum("bgrcqd,gh,ce->bgrcqhed", qb_r, eye2, eye2).reshape(nbs, N_KV_B * rb * 2 * dseq, N_KV_B * 2 * HEAD_DIM)

    def new_rows(a):
        return jnp.pad(a.reshape(nbs, dseq, -1), ((0, 0), (0, LANES - dseq), (0, 0)))

    oa_s, ob_s = _sample_call(
        page_table, qi_s, qa_s, qb_s, ps["kiwi"].reshape(nbs, dseq, LANES),
        new_rows(ps["kib"][:, :IDX_DIM]), new_rows(ps["kab"]), new_rows(ps["vab"]), new_rows(ps["kbb"]), new_rows(ps["vbb"]),
        lam4, subln,
        cache_k_idx[0], cache_k_a[0].reshape(-1, PAGE, KV_A), cache_v_a[0].reshape(-1, PAGE, KV_A),
        cache_k_b[0].reshape(-1, PAGE, KV_B), cache_v_b[0].reshape(-1, PAGE, KV_B), lam_init, ppc=ppc)
    oa_s = oa_s.reshape(ns_tok, Q_A).astype(BF16)
    ob_s = ob_s.reshape(ns_tok, Q_B).astype(BF16)

    x1_p, h2_p, lg_p = _outproj_call(oa_p, ob_p, pp["sga"], pp["sgb"], xp, mod_p[2], mod_p[4], mod_p[3],
                                     wa, wb, wo, l1g, l1b, wrh, wrl, br, alpha, tm=tm, mod_map=pmap)
    x1_s, h2_s, lg_s = _outproj_call(oa_s, ob_s, ps["sga"], ps["sgb"], xs, mod_s[2], mod_s[4], mod_s[3],
                                     wa, wb, wo, l1g, l1b, wrh, wrl, br, alpha, tm=ns_tok, mod_map=smap)

    nblk_p = np_tok // tb
    xs_all, route_p, cnt_p = _dispatch_call(lg_p, h2_p, tb=tb, slots=slots, total_blocks=nblk_p + 1)
    xs_all, route_s, cnt_s = _dispatch_call(lg_s, h2_s, tb=ns_tok, slots=slots, total_blocks=nblk_p + 1,
                                            blk0=nblk_p, xs_buf=xs_all)
    cnt = jnp.concatenate([cnt_p.reshape(nblk_p, 8, LANES)[:, 0, :N_EXPERTS],
                           cnt_s.reshape(1, 8, LANES)[:, 0, :N_EXPERTS]], axis=0).astype(I32)
    total_units = (np_tok + ns_tok) * TOP_K_EXPERTS // UNIT + (nblk_p + 1) * N_EXPERTS
    max_tiles = total_units // TILE_UNITS + N_EXPERTS
    te, src, n_tiles = _unit_tables(cnt, upb, max_tiles)
    ys = _ffn_call(te, src, n_tiles, xs_all.reshape(-1, UNIT, D_MODEL), wg, bg, wu, bu, wd, bd).reshape(-1, D_MODEL)
    y_p = _combine_call(ys, route_p, x1_p, mod_p[5], l2g, l2b, alpha, tb=tb, slots=slots, blk0=0, mod_map=pmap)
    y_s = _combine_call(ys, route_s, x1_s, mod_s[5], l2g, l2b, alpha, tb=ns_tok, slots=slots, blk0=nblk_p, mod_map=smap)

    st = lambda a, nb_, t_, shp: a.reshape((1, nb_, t_) + shp)
    outs = [y_p.reshape(nbp, seq, D_MODEL), y_s.reshape(nbs, dseq, D_MODEL)]
    for p_, nb_, t_ in ((pp, nbp, seq), (ps, nbs, dseq)):
        outs += [st(p_["ka"], nb_, t_, (N_KV_A, HEAD_DIM)), st(p_["va"], nb_, t_, (N_KV_A, HEAD_DIM)),
                 st(p_["ki"], nb_, t_, (IDX_DIM,)), st(p_["kb"], nb_, t_, (N_KV_B, 2, HEAD_DIM)),
                 st(p_["vb"], nb_, t_, (N_KV_B, 2 * HEAD_DIM))]
    return tuple(outs)
```

```python
import functools
import math

import jax
import jax.numpy as jnp
import numpy as np
from jax import lax
from jax.experimental import pallas as pl
from jax.experimental.pallas import tpu as pltpu

F32, BF16, I32 = jnp.float32, jnp.bfloat16, jnp.int32

D_MODEL = 1024
HEAD_DIM = 64
N_HEADS_A, N_KV_A = 8, 2
N_IDX_HEADS, IDX_DIM = 8, 64
TOPK_MAX = 256
N_HEADS_B, N_KV_B = 4, 2
N_EXPERTS, TOP_K_EXPERTS, D_FF = 32, 4, 1024
SWIGLU_ALPHA, SWIGLU_LIMIT = 1.702, 7.0
ROPE_THETA = 10000.0
LN_EPS = 1e-5
PAGE = 128
Q_A, KV_A = N_HEADS_A * HEAD_DIM, N_KV_A * HEAD_DIM
Q_IDX = N_IDX_HEADS * IDX_DIM
Q_B, KV_B = N_HEADS_B * 2 * HEAD_DIM, N_KV_B * 2 * HEAD_DIM
PROJ_SIZES = (Q_A, KV_A, KV_A, Q_IDX, IDX_DIM, N_IDX_HEADS, Q_B, KV_B, KV_B, D_MODEL, D_MODEL)
IDX_SCALE = (N_IDX_HEADS * IDX_DIM) ** -0.5
QK_SCALE = HEAD_DIM ** -0.5

LANES = 128
UNIT = 16
TILE_UNITS = 16
TILE_M = UNIT * TILE_UNITS
INT_MIN = -(2 ** 31)
NEG_INF = float("-inf")
VMEM_LIMIT = 56 * 1024 * 1024

C_QA, C_QI, C_QB = 0, 512, 1024
C_KA, C_VA, C_KB, C_VB, C_KIWI, C_GA, C_GB = 1536, 1664, 1792, 2048, 2304, 2432, 3456
W_IN_COLS = 4480


def _params(sem=None):
    return pltpu.CompilerParams(dimension_semantics=sem, vmem_limit_bytes=VMEM_LIMIT)


def _ln(xf):
    mu = jnp.mean(xf, axis=-1, keepdims=True)
    xc = xf - mu
    var = jnp.mean(xc * xc, axis=-1, keepdims=True)
    return xc * lax.rsqrt(var + LN_EPS)


def _dot(a, b):
    return jnp.dot(a, b, preferred_element_type=F32)


def _dot_nt(a, b):
    return lax.dot_general(a, b, (((1,), (1,)), ((), ())), preferred_element_type=F32)


def _ada_kernel(c_ref, w_ref, b_ref, o_ref):
    c = c_ref[...]
    s = c * jax.nn.sigmoid(c)
    o_ref[...] = jnp.dot(s, w_ref[...], preferred_element_type=F32,
                         precision=lax.Precision.HIGHEST) + b_ref[...]


def _ada_call(c, w, b):
    n = c.shape[0]
    tn = 1024
    return pl.pallas_call(
        _ada_kernel,
        out_shape=jax.ShapeDtypeStruct((n, 6 * D_MODEL), F32),
        grid=(6 * D_MODEL // tn,),
        in_specs=[pl.BlockSpec((n, D_MODEL), lambda j: (0, 0)),
                  pl.BlockSpec((D_MODEL, tn), lambda j: (0, j)),
                  pl.BlockSpec((1, tn), lambda j: (0, j))],
        out_specs=pl.BlockSpec((n, tn), lambda j: (0, j)),
        compiler_params=_params(("arbitrary",)),
        name="ada",
    )(c, w, b.reshape(1, -1))


_PROJ_OUTS = (
    ("qa", 512, BF16), ("qi", 512, BF16), ("qb", 512, BF16),
    ("ka", 128, F32), ("va", 128, F32), ("kb", 256, F32), ("vb", 256, F32), ("ki", 64, F32),
    ("kab", 128, BF16), ("vab", 128, BF16), ("kbb", 256, BF16), ("vbb", 256, BF16), ("kib", 128, BF16),
    ("kiwi", 128, F32), ("sga", 1024, BF16), ("sgb", 1024, BF16),
)


_STATE_T = ("ka", "va", "kb", "ki")


def _proj_kernel(state_t, x_ref, sc_ref, sh_ref, w_ref, cos_ref, sin_ref, *outs):
    o = dict(zip([n for n, _, _ in _PROJ_OUTS], outs))
    tm = x_ref.shape[0]

    def put_state(name, v, c0=0):
        wd = min(o[name].shape[-2 if state_t else -1] - c0, LANES)
        if state_t:
            o[name][0, c0:c0 + wd, :] = v.T[0:wd]
        else:
            o[name][:, c0:c0 + wd] = v[:, 0:wd]

    h = (_ln(x_ref[...]) * (1.0 + sc_ref[0]) + sh_ref[0]).astype(BF16)
    cos, sin = cos_ref[...], sin_ref[...]
    lane = lax.broadcasted_iota(I32, (tm, LANES), 1)
    lo_half = (lane % HEAD_DIM) < (HEAD_DIM // 2)

    def rope(v):
        sw = jnp.where(lo_half, pltpu.roll(v, LANES - HEAD_DIM // 2, 1), pltpu.roll(v, HEAD_DIM // 2, 1))
        return v * cos + sw * sin

    def proj(c0, width):
        return _dot(h, w_ref[:, c0:c0 + width])

    for name, c0 in (("qa", C_QA), ("qi", C_QI), ("qb", C_QB)):
        r = proj(c0, 512)
        for j in range(4):
            o[name][:, j * LANES:(j + 1) * LANES] = rope(r[:, j * LANES:(j + 1) * LANES]).astype(BF16)
    r = rope(proj(C_KA, 128))
    put_state("ka", r)
    o["kab"][...] = r.astype(BF16)
    r = proj(C_VA, 128)
    put_state("va", r)
    o["vab"][...] = r.astype(BF16)
    r = proj(C_KB, 256)
    for j in range(2):
        rr = rope(r[:, j * LANES:(j + 1) * LANES])
        put_state("kb", rr, j * LANES)
        o["kbb"][:, j * LANES:(j + 1) * LANES] = rr.astype(BF16)
    r = proj(C_VB, 256)
    o["vb"][...] = r
    o["vbb"][...] = r.astype(BF16)
    r = proj(C_KIWI, 128)
    rr = rope(r)
    put_state("ki", rr)
    o["kib"][...] = rr.astype(BF16)
    o["kiwi"][...] = jnp.where(lane < IDX_DIM, rr, r * IDX_SCALE)
    o["sga"][...] = jax.nn.sigmoid(proj(C_GA, D_MODEL)).astype(BF16)
    o["sgb"][...] = jax.nn.sigmoid(proj(C_GB, D_MODEL)).astype(BF16)


def _proj_call(x, sc, sh, w, cos, sin, *, tm, mod_map, pos_map, seq=None):
    n = x.shape[0]
    mod_rows = sc.shape[1]
    row = lambda i: (i, 0)
    out_shape, out_specs = [], []
    for name, wd, dt in _PROJ_OUTS:
        if seq is not None and name in _STATE_T:
            out_shape.append(jax.ShapeDtypeStruct((n // seq, wd, seq), dt))
            out_specs.append(pl.BlockSpec((1, wd, tm), lambda i: (i // (seq // tm), 0, i % (seq // tm))))
        else:
            out_shape.append(jax.ShapeDtypeStruct((n, wd), dt))
            out_specs.append(pl.BlockSpec((tm, wd), row))
    return pl.pallas_call(
        functools.partial(_proj_kernel, seq is not None),
        out_shape=out_shape,
        grid=(n // tm,),
        in_specs=[pl.BlockSpec((tm, D_MODEL), row),
                  pl.BlockSpec((1, mod_rows, D_MODEL), mod_map),
                  pl.BlockSpec((1, mod_rows, D_MODEL), mod_map),
                  pl.BlockSpec((D_MODEL, W_IN_COLS), lambda i: (0, 0)),
                  pl.BlockSpec((tm, LANES), pos_map),
                  pl.BlockSpec((tm, LANES), pos_map)],
        out_specs=out_specs,
        compiler_params=_params(("arbitrary",)),
        name="proj",
    )(x, sc, sh, w, cos, sin)


def _sortable_key(score):
    bits = pltpu.bitcast(score + 0.0, I32)
    return jnp.where(bits < 0, bits ^ jnp.int32(0x7FFFFFFF), bits)


def _kth_largest_key(count_ge, rows, k):
    t0 = jnp.where(count_ge(jnp.zeros((rows, 1), I32)) >= k, jnp.int32(0), jnp.int32(INT_MIN))

    def body(i, t):
        cand = t | jnp.left_shift(jnp.int32(1), 30 - i)
        return jnp.where(count_ge(cand) >= k, cand, t)

    return lax.fori_loop(0, 31, body, t0)


def _lambda_full(lam_ref, lam_init):
    l = lam_ref[...]
    a = jnp.exp(jnp.sum(l[0:1] * l[1:2], axis=-1, keepdims=True))
    b = jnp.exp(jnp.sum(l[2:3] * l[3:4], axis=-1, keepdims=True))
    return a - b + lam_init


def _attn_kernel(tq, topk, lam_init, qa_ref, qi_ref, qb_ref, kiwi_ref, kab_ref, vab_ref, kbb_ref, vbb_ref,
                 kib_ref, lam_ref, sub_ref, oa_ref, ob_ref,
                 key_ref, bias_ref, s_ref, qs_ref, qas_ref, qbs_ref, wb_ref, va1_ref, vb1_ref, m_ref,
                 acca_ref, accb_ref):
    i = pl.program_id(1)
    kc = tq
    n_ch = i + 1
    tiles = kc // LANES
    seq = kab_ref.shape[0]
    ra, rb = N_HEADS_A // N_KV_A, N_HEADS_B // N_KV_B
    row = lax.broadcasted_iota(I32, (tq, kc), 0)
    col = lax.broadcasted_iota(I32, (tq, kc), 1)

    def wide(x):
        return jnp.concatenate([x] * tiles, axis=1)

    def keys_of(c):
        return pl.ds(pl.multiple_of(c * kc, kc), kc)

    @pl.when(i == 0)
    def _():
        one = jnp.where(lax.broadcasted_iota(I32, (seq, HEAD_DIM), 1) == 0, 1.0, 0.0).astype(BF16)
        for g in range(N_KV_A):
            va1_ref[:, g * LANES:(g + 1) * LANES] = jnp.concatenate(
                [vab_ref[:, g * HEAD_DIM:(g + 1) * HEAD_DIM], one], axis=1)
        one = jnp.where(lax.broadcasted_iota(I32, (seq, LANES), 1) == 0, 1.0, 0.0).astype(BF16)
        for g in range(N_KV_B):
            vb1_ref[:, g * 2 * LANES:g * 2 * LANES + LANES] = vbb_ref[:, g * LANES:(g + 1) * LANES]
            vb1_ref[:, g * 2 * LANES + LANES:(g + 1) * 2 * LANES] = one

    kiw = kiwi_ref[...]
    for h in range(N_IDX_HEADS):
        qs_ref[h * tq:(h + 1) * tq, :] = qi_ref[:, h * IDX_DIM:(h + 1) * IDX_DIM]
        wb_ref[h] = jnp.broadcast_to(kiw[:, IDX_DIM + h:IDX_DIM + h + 1], (tq, LANES))

    def idx_chunk(c, carry):
        s = _dot_nt(qs_ref[...], kib_ref[keys_of(c), 0:IDX_DIM])
        score = jnp.zeros((tq, kc), F32)
        for h in range(N_IDX_HEADS):
            score = score + wide(wb_ref[h]) * jnp.maximum(s[h * tq:(h + 1) * tq], 0.0)
        adm = col <= row + (i - c) * kc
        key_ref[c] = jnp.where(adm, _sortable_key(score), jnp.int32(INT_MIN))
        return carry

    lax.fori_loop(0, n_ch, idx_chunk, 0)

    def count(pred, t, r0, nr):
        tb = jnp.broadcast_to(t, (nr, LANES))

        def body(c, acc):
            for n in range(tiles):
                acc = acc + jnp.where(pred(key_ref[c, r0:r0 + nr, n * LANES:(n + 1) * LANES], tb), 1.0, 0.0)
            return acc

        acc = lax.fori_loop(0, n_ch, body, jnp.zeros((nr, LANES), F32))
        return jnp.sum(acc, axis=-1, keepdims=True)

    ge = lambda k, t: k >= t
    hr = tq // 2
    thr = jnp.concatenate(
        [_kth_largest_key(lambda t, r0=r0: count(ge, t, r0, hr), hr, float(topk)) for r0 in (0, hr)], axis=0)
    n_ge = count(ge, thr, 0, tq)

    def bias_chunk(c, carry):
        k = key_ref[c]
        bias_ref[c] = jnp.where((k >= thr) & (k > INT_MIN), 0.0, NEG_INF)
        return carry

    lax.fori_loop(0, n_ch, bias_chunk, 0)
    tie = jnp.max(jnp.where((thr > INT_MIN) & (n_ge > float(topk)), 1.0, 0.0))

    @pl.when(tie > 0.0)
    def _():
        need = float(topk) - count(lambda k, t: k > t, thr, 0, tq)
        before = jnp.where(lax.broadcasted_iota(I32, (kc, kc), 0) < lax.broadcasted_iota(I32, (kc, kc), 1), 1.0, 0.0)

        def fix(c, seen):
            k = key_ref[c]
            eq = k == thr
            eqf = jnp.where(eq, 1.0, 0.0)
            prefix = _dot(eqf.astype(BF16), before.astype(BF16)) + seen
            sel = ((k > thr) | (eq & (prefix < need))) & (k > INT_MIN)
            bias_ref[c] = jnp.where(sel, 0.0, NEG_INF)
            return seen + jnp.sum(eqf, axis=-1, keepdims=True)

        lax.fori_loop(0, n_ch, fix, jnp.zeros((tq, 1), F32))

    def fold(x, op):
        r = x[:, 0:LANES]
        for t in range(1, tiles):
            r = op(r, x[:, t * LANES:(t + 1) * LANES])
        return r

    def mixer(stacks, bias_of, values, acc_ref):
        nr = stacks[0][0].shape[0]
        n_rows = nr * len(stacks)
        m_ref[0:n_rows, :] = jnp.full((n_rows, LANES), NEG_INF, F32)

        def pass1(c, carry):
            ks = keys_of(c)
            bias = jnp.concatenate([bias_of(c)] * (nr // tq), axis=0)
            for n, (q, k_ref, k_col) in enumerate(stacks):
                s = _dot_nt(q[...], k_ref[ks, k_col:k_col + HEAD_DIM]) + bias
                s_ref[c, n * nr:(n + 1) * nr, :] = s
                m_ref[n * nr:(n + 1) * nr, :] = jnp.maximum(m_ref[n * nr:(n + 1) * nr, :], fold(s, jnp.maximum))
            return carry

        lax.fori_loop(0, n_ch, pass1, 0)
        m_ref[0:n_rows, :] = jnp.broadcast_to(jnp.max(m_ref[0:n_rows, :], axis=-1, keepdims=True), (n_rows, LANES))
        acc_ref[...] = jnp.zeros_like(acc_ref)

        def pass2(c, carry):
            ks = keys_of(c)
            for n0, cnt, v_ref, v_col, vw in values:
                r0, r1 = n0 * nr, (n0 + cnt) * nr
                p = jnp.exp(s_ref[c, r0:r1, :] - wide(m_ref[r0:r1, :]))
                acc_ref[r0:r1, :] += _dot(p.astype(BF16), v_ref[ks, v_col:v_col + vw])
            return carry

        lax.fori_loop(0, n_ch, pass2, 0)

    for g in range(N_KV_A):
        for r in range(ra):
            qas_ref[g, r * tq:(r + 1) * tq, :] = qa_ref[:, (g * ra + r) * HEAD_DIM:(g * ra + r + 1) * HEAD_DIM]
    mixer([(qas_ref.at[g], kab_ref, g * HEAD_DIM) for g in range(N_KV_A)],
          lambda c: bias_ref[c],
          [(g, 1, va1_ref, g * LANES, LANES) for g in range(N_KV_A)], acca_ref)
    for g in range(N_KV_A):
        o = acca_ref[g * ra * tq:(g + 1) * ra * tq, :]
        o = o[:, 0:HEAD_DIM] / o[:, HEAD_DIM:HEAD_DIM + 1]
        oa_ref[:, g * ra * HEAD_DIM:(g + 1) * ra * HEAD_DIM] = jnp.concatenate(
            [o[r * tq:(r + 1) * tq] for r in range(ra)], axis=-1).astype(BF16)

    for g in range(N_KV_B):
        for mp in range(2):
            for r in range(rb):
                qc = ((g * rb + r) * 2 + mp) * HEAD_DIM
                qbs_ref[g * 2 + mp, r * tq:(r + 1) * tq, :] = qb_ref[:, qc:qc + HEAD_DIM]
    mixer([(qbs_ref.at[n], kbb_ref, n * HEAD_DIM) for n in range(2 * N_KV_B)],
          lambda c: jnp.where(col <= row + (i - c) * kc, 0.0, NEG_INF),
          [(2 * g, 2, vb1_ref, g * 2 * LANES, 2 * LANES) for g in range(N_KV_B)], accb_ref)
    lam = _lambda_full(lam_ref, lam_init)
    for g in range(N_KV_B):
        o = accb_ref[g * 2 * rb * tq:(g + 1) * 2 * rb * tq, :]
        o = o[:, 0:LANES] / o[:, LANES:LANES + 1]
        for r in range(rb):
            d = o[r * tq:(r + 1) * tq] - lam * o[(rb + r) * tq:(rb + r + 1) * tq]
            d = d * lax.rsqrt(jnp.mean(d * d, axis=-1, keepdims=True) + LN_EPS) * sub_ref[...] * (1.0 - lam_init)
            oc0 = (g * rb + r) * 2 * HEAD_DIM
            ob_ref[:, oc0:oc0 + 2 * HEAD_DIM] = d.astype(BF16)


def _attn_call(p, lam4, subln, n_batch, seq, lam_init, *, tq):
    nq = seq // tq
    topk = min(TOPK_MAX, seq // 4)
    n = n_batch * seq
    qmap = lambda b, i: (b * nq + i, 0)
    kmap = lambda b, i: (b, 0)
    const = lambda b, i: (0, 0)
    return pl.pallas_call(
        functools.partial(_attn_kernel, tq, topk, lam_init),
        out_shape=[jax.ShapeDtypeStruct((n, Q_A), BF16), jax.ShapeDtypeStruct((n, Q_B), BF16)],
        grid=(n_batch, nq),
        in_specs=[pl.BlockSpec((tq, 512), qmap), pl.BlockSpec((tq, 512), qmap), pl.BlockSpec((tq, 512), qmap),
                  pl.BlockSpec((tq, LANES), qmap),
                  pl.BlockSpec((seq, 128), kmap), pl.BlockSpec((seq, 128), kmap),
                  pl.BlockSpec((seq, 256), kmap), pl.BlockSpec((seq, 256), kmap),
                  pl.BlockSpec((seq, 128), kmap),
                  pl.BlockSpec((4, HEAD_DIM), const), pl.BlockSpec((1, 2 * HEAD_DIM), const)],
        out_specs=[pl.BlockSpec((tq, Q_A), qmap), pl.BlockSpec((tq, Q_B), qmap)],
        scratch_shapes=[pltpu.VMEM((nq, tq, tq), I32), pltpu.VMEM((nq, tq, tq), F32),
                        pltpu.VMEM((nq, N_HEADS_A * tq, tq), F32),
                        pltpu.VMEM((N_IDX_HEADS * tq, HEAD_DIM), BF16),
                        pltpu.VMEM((N_KV_A, (N_HEADS_A // N_KV_A) * tq, HEAD_DIM), BF16),
                        pltpu.VMEM((2 * N_KV_B, (N_HEADS_B // N_KV_B) * tq, HEAD_DIM), BF16),
                        pltpu.VMEM((N_IDX_HEADS, tq, LANES), F32),
                        pltpu.VMEM((seq, N_KV_A * LANES), BF16), pltpu.VMEM((seq, N_KV_B * 2 * LANES), BF16),
                        pltpu.VMEM((N_HEADS_A * tq, LANES), F32),
                        pltpu.VMEM((N_HEADS_A * tq, LANES), F32), pltpu.VMEM((2 * N_HEADS_B * tq, 2 * LANES), F32)],
        compiler_params=_params(("arbitrary", "arbitrary")),
        name="attn",
    )(p["qa"], p["qi"], p["qb"], p["kiwi"], p["kab"], p["vab"], p["kbb"], p["vbb"], p["kib"], lam4, subln)


def _sample_kernel(n_chunks, ppc, topk, lam_init,
                   pt_ref, qi_ref, qa_ref, qb_ref, wi_ref, kin_ref, kan_ref, van_ref, kbn_ref, vbn_ref,
                   lam_ref, sub_ref, cki_ref, cka_ref, cva_ref, ckb_ref, cvb_ref,
                   oa_ref, ob_ref,
                   b64, b128, b256, bvb, sem, key_ref, bias_ref, s_ref, acc_ref, l_ref):
    b = pl.program_id(0)
    ch = ppc * PAGE
    past = n_chunks * ch
    width = past + LANES
    nq = 8

    def stream(cache, buf, fn):
        transposed = buf is not bvb
        page_rows = cache.shape[1]

        def copy(c, slot, pg):
            dst = (buf.at[slot, :, pl.ds(pg * PAGE, PAGE)] if transposed
                   else buf.at[slot, pl.ds(pg * page_rows, page_rows)])
            return pltpu.make_async_copy(cache.at[pt_ref[b, c * ppc + pg]], dst, sem.at[slot])

        def start(c, slot):
            for pg in range(ppc):
                copy(c, slot, pg).start()

        start(0, 0)

        def body(c, carry):
            slot = c % 2

            @pl.when(c + 1 < n_chunks)
            def _():
                start(c + 1, 1 - slot)

            for pg in range(ppc):
                copy(c, slot, pg).wait()
            fn(c, buf.at[slot])
            return carry

        lax.fori_loop(0, n_chunks, body, 0)

    def chunk_ds(c):
        return pl.ds(pl.multiple_of(c * ch, ch), ch)

    newcol = lax.broadcasted_iota(I32, (nq, LANES), 1)
    newrow = lax.broadcasted_iota(I32, (nq, LANES), 0)
    new_adm = newcol <= newrow

    wi = wi_ref[0]

    def head_sum(s):
        acc = jnp.zeros((nq, s.shape[1]), F32)
        for h in range(N_IDX_HEADS):
            acc = acc + wi[:, IDX_DIM + h:IDX_DIM + h + 1] * jnp.maximum(s[h * nq:(h + 1) * nq], 0.0)
        return acc

    def idx_fn(c, kbuf):
        s = _dot(qi_ref[0], kbuf[...].astype(BF16))
        key_ref[:, chunk_ds(c)] = _sortable_key(head_sum(s))

    stream(cki_ref, b64, idx_fn)
    s_new = head_sum(_dot_nt(qi_ref[0], kin_ref[0]))
    key_ref[:, past:width] = jnp.where(new_adm, _sortable_key(s_new), jnp.int32(INT_MIN))

    def count_ge(t):
        return jnp.sum(jnp.where(key_ref[...] >= t, 1.0, 0.0), axis=-1, keepdims=True)

    thr = _kth_largest_key(count_ge, nq, float(topk))
    key = key_ref[...]
    n_ge = count_ge(thr)
    bias_ref[...] = jnp.where((key >= thr) & (key > INT_MIN), 0.0, NEG_INF)
    tie = jnp.max(jnp.where((thr > INT_MIN) & (n_ge > float(topk)), 1.0, 0.0))

    @pl.when(tie > 0.0)
    def _():
        need = float(topk) - jnp.sum(jnp.where(key_ref[...] > thr, 1.0, 0.0), axis=-1, keepdims=True)

        def fix(lo, n, seen):
            k2 = key_ref[:, pl.ds(lo, n)]
            eq = k2 == thr
            before = lax.broadcasted_iota(I32, (n, n), 0) < lax.broadcasted_iota(I32, (n, n), 1)
            eqf = jnp.where(eq, 1.0, 0.0)
            prefix = _dot(eqf.astype(BF16), jnp.where(before, 1.0, 0.0).astype(BF16)) + seen
            sel = ((k2 > thr) | (eq & (prefix < need))) & (k2 > INT_MIN)
            bias_ref[:, pl.ds(lo, n)] = jnp.where(sel, 0.0, NEG_INF)
            return seen + jnp.sum(eqf, axis=-1, keepdims=True)

        seen = lax.fori_loop(0, n_chunks, lambda c, sn: fix(pl.multiple_of(c * ch, ch), ch, sn),
                             jnp.zeros((nq, 1), F32))
        fix(past, LANES, seen)

    def attend(q_ref, ck_ref, kbuf, k_new_ref, cv_ref, vbuf, add_values, add_new_values, bias_of):
        groups = q_ref.shape[1] // nq

        def tile_rows(x):
            return jnp.concatenate([x] * groups, axis=0)

        def k_fn(c, kb):
            s_ref[:, chunk_ds(c)] = _dot(q_ref[0], kb[...].astype(BF16)) + tile_rows(bias_of(chunk_ds(c), ch))

        stream(ck_ref, kbuf, k_fn)
        s_ref[:, past:width] = _dot_nt(q_ref[0], k_new_ref[0]) + tile_rows(bias_of(pl.ds(past, LANES), LANES))
        m = jnp.max(s_ref[...], axis=-1, keepdims=True)
        acc_ref[...] = jnp.zeros_like(acc_ref)
        l_ref[...] = jnp.zeros_like(l_ref)

        def v_fn(c, vb):
            p = jnp.exp(s_ref[:, chunk_ds(c)] - m)
            l_ref[...] += jnp.sum(p, axis=-1, keepdims=True)
            add_values(p.astype(BF16), vb)

        stream(cv_ref, vbuf, v_fn)
        p = jnp.exp(s_ref[:, past:width] - m)
        l_ref[...] += jnp.sum(p, axis=-1, keepdims=True)
        add_new_values(p.astype(BF16))
        return acc_ref[...] / l_ref[:, 0:1]

    def add_a(p, vb):
        acc_ref[...] += _dot_nt(p, vb[...].astype(BF16))

    def add_a_new(p):
        acc_ref[...] += _dot(p, van_ref[0])

    o = attend(qa_ref, cka_ref, b128, kan_ref, cva_ref, b128, add_a, add_a_new, lambda ds, n: bias_ref[:, ds])
    ra = N_HEADS_A // N_KV_A
    for g in range(N_KV_A):
        for r in range(ra):
            hh = g * ra + r
            oa_ref[0, :, hh * HEAD_DIM:(hh + 1) * HEAD_DIM] = o[hh * nq:(hh + 1) * nq, g * HEAD_DIM:(g + 1) * HEAD_DIM]

    cb_new = jnp.where(new_adm, 0.0, NEG_INF)
    rb = N_HEADS_B // N_KV_B
    grows = rb * 2 * nq

    def bias_b(ds, n):
        return cb_new if n == LANES else jnp.zeros((nq, n), F32)

    def add_b(p, vb):
        for g in range(N_KV_B):
            vg = vb[pl.ds(g, ch, stride=N_KV_B), :].astype(BF16)
            acc_ref[g * grows:(g + 1) * grows, :] += _dot(p[g * grows:(g + 1) * grows], vg)

    def add_b_new(p):
        for g in range(N_KV_B):
            acc_ref[g * grows:(g + 1) * grows, :] += _dot(p[g * grows:(g + 1) * grows],
                                                           vbn_ref[0, :, g * 2 * HEAD_DIM:(g + 1) * 2 * HEAD_DIM])

    o = attend(qb_ref, ckb_ref, b256, kbn_ref, cvb_ref, bvb, add_b, add_b_new, bias_b)
    lam = _lambda_full(lam_ref, lam_init)
    for g in range(N_KV_B):
        for r in range(rb):
            base = (g * rb + r) * 2 * nq
            cols = slice(0, 2 * HEAD_DIM)
            d = o[base:base + nq, cols] - lam * o[base + nq:base + 2 * nq, cols]
            d = d * lax.rsqrt(jnp.mean(d * d, axis=-1, keepdims=True) + LN_EPS) * sub_ref[...] * (1.0 - lam_init)
            oc0 = (g * rb + r) * 2 * HEAD_DIM
            ob_ref[0, :, oc0:oc0 + 2 * HEAD_DIM] = d


def _sample_call(page_table, qi, qa, qb, wi, kin, kan, van, kbn, vbn, lam4, subln,
                 cki, cka, cva, ckb, cvb, lam_init, *, ppc):
    nb, n_pages = page_table.shape
    n_chunks = n_pages // ppc
    ch = ppc * PAGE
    past = n_pages * PAGE
    width = past + LANES
    topk = min(TOPK_MAX, (past + 8) // 4)
    b3 = lambda b, pt: (b, 0, 0)
    const = lambda b, pt: (0, 0)
    any_spec = pl.BlockSpec(memory_space=pl.ANY)
    full = lambda a: pl.BlockSpec((1,) + a.shape[1:], b3)
    grid_spec = pltpu.PrefetchScalarGridSpec(
        num_scalar_prefetch=1,
        grid=(nb,),
        in_specs=[full(qi), full(qa), full(qb), full(wi), full(kin), full(kan), full(van), full(kbn), full(vbn),
                  pl.BlockSpec((4, HEAD_DIM), const), pl.BlockSpec((1, 2 * HEAD_DIM), const),
                  any_spec, any_spec, any_spec, any_spec, any_spec],
        out_specs=[pl.BlockSpec((1, 8, Q_A), b3), pl.BlockSpec((1, 8, Q_B), b3)],
        scratch_shapes=[pltpu.VMEM((2, IDX_DIM, ch), F32), pltpu.VMEM((2, KV_A, ch), F32), pltpu.VMEM((2, KV_B, ch), F32),
                        pltpu.VMEM((2, N_KV_B * ch, 2 * HEAD_DIM), F32),
                        pltpu.SemaphoreType.DMA((2,)),
                        pltpu.VMEM((8, width), I32), pltpu.VMEM((8, width), F32),
                        pltpu.VMEM((64, width), F32), pltpu.VMEM((64, LANES), F32), pltpu.VMEM((64, LANES), F32)],
    )
    return pl.pallas_call(
        functools.partial(_sample_kernel, n_chunks, ppc, topk, lam_init),
        out_shape=[jax.ShapeDtypeStruct((nb, 8, Q_A), F32), jax.ShapeDtypeStruct((nb, 8, Q_B), F32)],
        grid_spec=grid_spec,
        compiler_params=_params(("arbitrary",)),
        name="sample_mix",
    )(page_table, qi, qa, qb, wi, kin, kan, van, kbn, vbn, lam4, subln, cki, cka, cva, ckb, cvb)


def _split_bf16(x):
    hi = x.astype(BF16)
    return hi, (x - hi.astype(F32)).astype(BF16)


def _outproj_kernel(alpha, oa_ref, ob_ref, sga_ref, sgb_ref, x_ref, g1_ref, sc2_ref, sh2_ref,
                    wa_ref, wb_ref, wo_ref, lg_ref, lb_ref, wrh_ref, wrl_ref, br_ref,
                    x1_ref, h2_ref, lo_ref):
    merged = (sga_ref[...].astype(F32) * _dot(oa_ref[...], wa_ref[...])
              + sgb_ref[...].astype(F32) * _dot(ob_ref[...], wb_ref[...]))
    y = _dot(merged.astype(BF16), wo_ref[...])
    x1 = _ln(alpha * x_ref[...] + (1.0 + g1_ref[0]) * y) * lg_ref[...] + lb_ref[...]
    x1_ref[...] = x1
    h2 = _ln(x1) * (1.0 + sc2_ref[0]) + sh2_ref[0]
    h2_ref[...] = h2.astype(BF16)
    hi, lo = _split_bf16(h2)
    logits = _dot(hi, wrh_ref[...]) + (_dot(lo, wrh_ref[...]) + _dot(hi, wrl_ref[...])) + br_ref[...]
    lane = lax.broadcasted_iota(I32, logits.shape, 1)
    lo_ref[...] = jnp.where(lane < N_EXPERTS, logits, NEG_INF)


def _outproj_call(oa, ob, sga, sgb, x, g1, sc2, sh2, wa, wb, wo, lg, lb, wrh, wrl, br, alpha, *, tm, mod_map):
    n = x.shape[0]
    mod_rows = g1.shape[1]
    row = lambda i: (i, 0)
    const = lambda i: (0, 0)
    mod = pl.BlockSpec((1, mod_rows, D_MODEL), mod_map)
    return pl.pallas_call(
        functools.partial(_outproj_kernel, alpha),
        out_shape=[jax.ShapeDtypeStruct((n, D_MODEL), F32), jax.ShapeDtypeStruct((n, D_MODEL), BF16),
                   jax.ShapeDtypeStruct((n, LANES), F32)],
        grid=(n // tm,),
        in_specs=[pl.BlockSpec((tm, 512), row), pl.BlockSpec((tm, 512), row),
                  pl.BlockSpec((tm, D_MODEL), row), pl.BlockSpec((tm, D_MODEL), row),
                  pl.BlockSpec((tm, D_MODEL), row), mod, mod, mod,
                  pl.BlockSpec((512, D_MODEL), const), pl.BlockSpec((512, D_MODEL), const),
                  pl.BlockSpec((D_MODEL, D_MODEL), const),
                  pl.BlockSpec((1, D_MODEL), const), pl.BlockSpec((1, D_MODEL), const),
                  pl.BlockSpec((D_MODEL, LANES), const), pl.BlockSpec((D_MODEL, LANES), const),
                  pl.BlockSpec((1, LANES), const)],
        out_specs=[pl.BlockSpec((tm, D_MODEL), row), pl.BlockSpec((tm, D_MODEL), row), pl.BlockSpec((tm, LANES), row)],
        compiler_params=_params(("arbitrary",)),
        name="outproj",
    )(oa, ob, sga, sgb, x, g1, sc2, sh2, wa, wb, wo, lg, lb, wrh, wrl, br)


def _dispatch_kernel(slots, n_real, lo_ref, h2_ref, *rest):
    xs_ref, route_ref, cnt_ref = rest[-3:]

    @pl.when(pl.program_id(0) < n_real)
    def _():
        _dispatch_block(slots, lo_ref, h2_ref, xs_ref, route_ref, cnt_ref)

    @pl.when(pl.program_id(0) >= n_real)
    def _():
        xs_ref[...] = jnp.zeros_like(xs_ref)


def _dispatch_block(slots, lo_ref, h2_ref, xs_ref, route_ref, cnt_ref):
    tb = lo_ref.shape[0]
    lane = lax.broadcasted_iota(I32, (tb, LANES), 1)
    l = lo_ref[...]
    vals, hots = [], []
    for _ in range(TOP_K_EXPERTS):
        m = jnp.max(l, axis=-1, keepdims=True)
        idx = jnp.min(jnp.where(l == m, lane, LANES), axis=-1, keepdims=True)
        hot = lane == idx
        vals.append(m)
        hots.append(hot)
        l = jnp.where(hot, NEG_INF, l)
    es = [jnp.exp(v - vals[0]) for v in vals]
    den = es[0] + es[1] + es[2] + es[3]
    ws = [e / den for e in es]
    sel = jnp.zeros((tb, LANES), F32)
    for hot in hots:
        sel = sel + jnp.where(hot, 1.0, 0.0)
    earlier = lax.broadcasted_iota(I32, (tb, tb), 1) < lax.broadcasted_iota(I32, (tb, tb), 0)
    pos = _dot(jnp.where(earlier, 1.0, 0.0).astype(BF16), sel.astype(BF16))
    cnt = jnp.sum(sel, axis=0, keepdims=True)
    run_units = jnp.right_shift(cnt.astype(I32) + (UNIT - 1), 4).astype(F32)
    lower = lax.broadcasted_iota(I32, (LANES, LANES), 0) < lax.broadcasted_iota(I32, (LANES, LANES), 1)
    off_units = _dot(jnp.broadcast_to(run_units, (8, LANES)).astype(BF16), jnp.where(lower, 1.0, 0.0).astype(BF16))
    dest = off_units[0:1] * float(UNIT) + pos
    route = jnp.zeros((tb, LANES), F32)
    for j in range(TOP_K_EXPERTS):
        dj = jnp.sum(jnp.where(hots[j], dest, 0.0), axis=-1, keepdims=True)
        route = route + jnp.where(lane == j, dj, 0.0) + jnp.where(lane == TOP_K_EXPERTS + j, ws[j], 0.0)
    route_ref[...] = route
    cnt_ref[...] = jnp.broadcast_to(cnt, (8, LANES))
    route_t = route.T
    slot = lax.broadcasted_iota(I32, (slots, tb), 0).astype(F32)
    hit = slot == route_t[0:1]
    for j in range(1, TOP_K_EXPERTS):
        hit = hit | (slot == route_t[j:j + 1])
    xs_ref[...] = _dot(jnp.where(hit, 1.0, 0.0).astype(BF16), h2_ref[...]).astype(BF16)


def _dispatch_call(logits, h2, *, tb, slots, total_blocks, blk0=0, xs_buf=None):
    n = h2.shape[0]
    nb = n // tb
    steps = nb if xs_buf is not None else total_blocks - blk0
    row = lambda i: (jnp.minimum(i, nb - 1), 0)
    args, in_specs, aliases = [logits, h2], [pl.BlockSpec((tb, LANES), row), pl.BlockSpec((tb, D_MODEL), row)], {}
    if xs_buf is not None:
        args.append(xs_buf)
        in_specs.append(pl.BlockSpec(memory_space=pl.ANY))
        aliases = {2: 0}
    return pl.pallas_call(
        functools.partial(_dispatch_kernel, slots, nb),
        out_shape=[jax.ShapeDtypeStruct((total_blocks * slots, D_MODEL), BF16), jax.ShapeDtypeStruct((n, LANES), F32),
                   jax.ShapeDtypeStruct((nb * 8, LANES), F32)],
        grid=(steps,),
        in_specs=in_specs,
        out_specs=[pl.BlockSpec((slots, D_MODEL), lambda i: (blk0 + i, 0)), pl.BlockSpec((tb, LANES), row),
                   pl.BlockSpec((8, LANES), row)],
        input_output_aliases=aliases,
        compiler_params=_params(("arbitrary",)),
        name="dispatch",
    )(*args)


def _ffn_kernel(te_ref, src_ref, nt_ref, xs_ref, wg_ref, bg_ref, wu_ref, bu_ref, wd_ref, bd_ref, ys_ref,
                xbuf, ybuf, insem, outsem):
    j = pl.program_id(0)
    nt = nt_ref[0]
    slot = j % 2

    def in_copy(tile, s, k):
        u = jnp.maximum(src_ref[tile * TILE_UNITS + k], 0)
        return pltpu.make_async_copy(xs_ref.at[u], xbuf.at[s, pl.ds(k * UNIT, UNIT)], insem.at[s])

    def out_copy(tile, s, k):
        u = jnp.maximum(src_ref[tile * TILE_UNITS + k], 0)
        return pltpu.make_async_copy(ybuf.at[s, pl.ds(k * UNIT, UNIT)], ys_ref.at[u], outsem.at[s])

    def for_valid_units(tile, fn):
        for k in range(TILE_UNITS):
            @pl.when(src_ref[tile * TILE_UNITS + k] >= 0)
            def _(k=k):
                fn(k)

    def start_in(tile, s):
        for_valid_units(tile, lambda k: in_copy(tile, s, k).start())

    def wait_out(tile, s):
        for_valid_units(tile, lambda k: out_copy(tile, s, k).wait())

    @pl.when(j == 0)
    def _():
        xbuf[...] = jnp.zeros_like(xbuf)

        @pl.when(nt > 0)
        def _():
            start_in(0, 0)

    @pl.when(j + 1 < nt)
    def _():
        start_in(j + 1, 1 - slot)

    @pl.when(j < nt)
    def _():
        for_valid_units(j, lambda k: in_copy(j, slot, k).wait())
        x = xbuf[slot]
        gate = jnp.minimum(_dot(x, wg_ref[0]) + bg_ref[0], SWIGLU_LIMIT)
        up = jnp.clip(_dot(x, wu_ref[0]) + bu_ref[0], -SWIGLU_LIMIT, SWIGLU_LIMIT)
        act = (up + 1.0) * gate * jax.nn.sigmoid(SWIGLU_ALPHA * gate)
        y = _dot(act.astype(BF16), wd_ref[0]) + bd_ref[0]

        @pl.when(j >= 2)
        def _():
            wait_out(j - 2, slot)

        ybuf[slot] = y.astype(BF16)
        for_valid_units(j, lambda k: out_copy(j, slot, k).start())

    @pl.when(j == pl.num_programs(0) - 1)
    def _():
        @pl.when(nt >= 2)
        def _():
            wait_out(nt - 2, nt % 2)

        @pl.when(nt >= 1)
        def _():
            wait_out(nt - 1, (nt - 1) % 2)


def _ffn_call(tile_expert, src_units, n_tiles, xs_units, wg, bg, wu, bu, wd, bd):
    n_units = xs_units.shape[0]
    max_tiles = tile_expert.shape[0]
    wmap = lambda j, te, src, nt: (te[j], 0, 0)
    any_spec = pl.BlockSpec(memory_space=pl.ANY)
    grid_spec = pltpu.PrefetchScalarGridSpec(
        num_scalar_prefetch=3,
        grid=(max_tiles,),
        in_specs=[any_spec,
                  pl.BlockSpec((1, D_MODEL, D_FF), wmap), pl.BlockSpec((1, 1, D_FF), wmap),
                  pl.BlockSpec((1, D_MODEL, D_FF), wmap), pl.BlockSpec((1, 1, D_FF), wmap),
                  pl.BlockSpec((1, D_FF, D_MODEL), wmap), pl.BlockSpec((1, 1, D_MODEL), wmap)],
        out_specs=any_spec,
        scratch_shapes=[pltpu.VMEM((2, TILE_M, D_MODEL), BF16), pltpu.VMEM((2, TILE_M, D_MODEL), BF16),
                        pltpu.SemaphoreType.DMA((2,)), pltpu.SemaphoreType.DMA((2,))],
    )
    return pl.pallas_call(
        _ffn_kernel,
        out_shape=jax.ShapeDtypeStruct((n_units, UNIT, D_MODEL), BF16),
        grid_spec=grid_spec,
        input_output_aliases={3: 0},
        compiler_params=_params(("arbitrary",)),
        name="ffn",
    )(tile_expert, src_units, n_tiles, xs_units, wg, bg, wu, bu, wd, bd)


def _unit_tables(cnt, units_per_block, max_tiles):
    nb = cnt.shape[0]
    run_units = (cnt + (UNIT - 1)) // UNIT
    blk_off = jnp.cumsum(run_units, axis=1) - run_units
    cum = jnp.cumsum(run_units, axis=0)
    tot = cum[-1]
    tiles = (tot + (TILE_UNITS - 1)) // TILE_UNITS
    tile_end = jnp.cumsum(tiles)
    n_tiles = tile_end[-1]
    j = jnp.arange(max_tiles, dtype=I32)
    jc = jnp.minimum(j, n_tiles - 1)
    te = jnp.minimum(jnp.sum(tile_end[None, :] <= jc[:, None], axis=1), N_EXPERTS - 1).astype(I32)
    tile_start = (tile_end - tiles)[te]
    rank = ((j - tile_start) * TILE_UNITS)[:, None] + jnp.arange(TILE_UNITS, dtype=I32)[None, :]
    cum_t = cum.T[te]
    blk = jnp.sum(cum_t[:, None, :] <= rank[:, :, None], axis=-1).astype(I32)
    blk_c = jnp.minimum(blk, nb - 1)
    before = jnp.take_along_axis(cum_t - run_units.T[te], blk_c, axis=1)
    unit = blk_c * units_per_block + blk_off[blk_c, te[:, None]] + (rank - before)
    valid = (rank < tot[te][:, None]) & (j < n_tiles)[:, None]
    src = jnp.where(valid, unit, -1).astype(I32).reshape(-1)
    return te, src, n_tiles.astype(I32).reshape(1)


def _combine_kernel(alpha, ys_ref, route_ref, x1_ref, g2_ref, lg_ref, lb_ref, y_ref):
    tb = x1_ref.shape[0]
    slots = ys_ref.shape[0]
    route = route_ref[...]
    slot = lax.broadcasted_iota(I32, (tb, slots), 1).astype(F32)
    pw = jnp.zeros((tb, slots), F32)
    for j in range(TOP_K_EXPERTS):
        pw = pw + jnp.where(slot == route[:, j:j + 1], route[:, TOP_K_EXPERTS + j:TOP_K_EXPERTS + j + 1], 0.0)
    moe = _dot(pw.astype(BF16), ys_ref[...])
    y_ref[...] = _ln(alpha * x1_ref[...] + (1.0 + g2_ref[0]) * moe) * lg_ref[...] + lb_ref[...]


def _combine_call(ys, route, x1, g2, lg, lb, alpha, *, tb, slots, blk0, mod_map):
    n = x1.shape[0]
    mod_rows = g2.shape[1]
    row = lambda i: (i, 0)
    const = lambda i: (0, 0)
    return pl.pallas_call(
        functools.partial(_combine_kernel, alpha),
        out_shape=jax.ShapeDtypeStruct((n, D_MODEL), F32),
        grid=(n // tb,),
        in_specs=[pl.BlockSpec((slots, D_MODEL), lambda i: (blk0 + i, 0)),
                  pl.BlockSpec((tb, LANES), row), pl.BlockSpec((tb, D_MODEL), row),
                  pl.BlockSpec((1, mod_rows, D_MODEL), mod_map),
                  pl.BlockSpec((1, D_MODEL), const), pl.BlockSpec((1, D_MODEL), const)],
        out_specs=pl.BlockSpec((tb, D_MODEL), row),
        compiler_params=_params(("arbitrary",)),
        name="combine",
    )(ys, route, x1, g2, lg, lb)


def _rope_tables(pos):
    half = HEAD_DIM // 2
    inv = ROPE_THETA ** (-np.arange(half, dtype=np.float64) / half)
    ang = np.asarray(pos, np.float64)[:, None] * inv[None, :]
    cos = np.tile(np.cos(ang), (1, LANES // half))
    sin = np.tile(np.concatenate([-np.sin(ang), np.sin(ang)], axis=1), (1, LANES // HEAD_DIM))
    return jnp.asarray(cos, F32), jnp.asarray(sin, F32)


def _prep_w_in(w):
    offs = np.concatenate([[0], np.cumsum(PROJ_SIZES)])
    qa, ka, va, qi, ki, wi, qb, kb, vb, ga, gb = [w[:, offs[i]:offs[i + 1]] for i in range(11)]
    pad = jnp.zeros((D_MODEL, LANES - IDX_DIM - N_IDX_HEADS), w.dtype)
    return jnp.concatenate([qa * QK_SCALE, qi, qb * QK_SCALE, ka, va, kb, vb, ki, wi, pad, ga, gb], axis=1).astype(BF16)


def _block_diag_rows(q, n_blocks):
    eye = jnp.eye(n_blocks, dtype=q.dtype)
    nb, _, rows, d = q.shape
    return jnp.einsum("bgrd,gh->bgrhd", q, eye).reshape(nb, n_blocks * rows, n_blocks * d)


def kernel(x_prompt, x_sample, cache_k_a, cache_v_a, cache_k_idx, cache_k_b, cache_v_b, page_table, c_prompt, c_sample, w_ada, b_ada, w_in, w_br_a, w_br_b, w_o, lambda_q1, lambda_k1, lambda_q2, lambda_k2, subln_g, ln1_g, ln1_b, ln2_g, ln2_b, w_router, b_router, w_gate, b_gate, w_up, b_up, w_down, b_down):
    nbp, seq, _ = x_prompt.shape
    nbs, dseq, _ = x_sample.shape
    depth = w_ada.shape[0]
    assert depth == 1 and dseq == 8
    n_pages = page_table.shape[1]
    past = n_pages * PAGE
    alpha = (2 * depth) ** 0.25
    lam_init = 0.8 - 0.6 * math.exp(-0.3 * 0)
    np_tok, ns_tok = nbp * seq, nbs * dseq

    tm = min(512, seq)
    tq = min(256, seq)
    tb = tm
    slots = ((4 * tb + N_EXPERTS * (UNIT - 1) + TILE_M - 1) // TILE_M) * TILE_M
    upb = slots // UNIT
    ppc = min(16, n_pages)

    w_in_r = _prep_w_in(w_in[0])
    wa, wb, wo = w_br_a[0].astype(BF16), w_br_b[0].astype(BF16), w_o[0].astype(BF16)
    wr = jnp.pad(w_router[0], ((0, 0), (0, LANES - N_EXPERTS)))
    wrh = wr.astype(BF16)
    wrl = (wr - wrh.astype(F32)).astype(BF16)
    br = jnp.pad(b_router[0], (0, LANES - N_EXPERTS)).reshape(1, LANES)
    wg, wu, wd = w_gate[0].astype(BF16), w_up[0].astype(BF16), w_down[0].astype(BF16)
    bg, bu, bd = b_gate[0][:, None, :], b_up[0][:, None, :], b_down[0][:, None, :]
    lam4 = jnp.stack([lambda_q1[0], lambda_k1[0], lambda_q2[0], lambda_k2[0]])
    subln = subln_g[0].reshape(1, -1)
    l1g, l1b, l2g, l2b = (a[0].reshape(1, -1) for a in (ln1_g, ln1_b, ln2_g, ln2_b))

    mod = _ada_call(jnp.concatenate([c_prompt, c_sample], axis=0), w_ada[0], b_ada[0])
    mod_p = [m.reshape(nbp, 1, D_MODEL) for m in jnp.split(mod[:nbp], 6, axis=-1)]
    mod_s = [jnp.repeat(m, dseq, axis=0).reshape(1, ns_tok, D_MODEL) for m in jnp.split(mod[nbp:], 6, axis=-1)]
    tiles_per_seq = seq // tm
    pmap = lambda i: (i // tiles_per_seq, 0, 0)
    smap = lambda i: (0, 0, 0)

    cos_p, sin_p = _rope_tables(np.arange(seq))
    cos_s, sin_s = _rope_tables(np.tile(past + np.arange(dseq), nbs))
    xp = x_prompt.reshape(np_tok, D_MODEL)
    xs = x_sample.reshape(ns_tok, D_MODEL)
    names = [n for n, _, _ in _PROJ_OUTS]
    pp = dict(zip(names, _proj_call(xp, mod_p[1], mod_p[0], w_in_r, cos_p, sin_p, tm=tm, mod_map=pmap,
                                    pos_map=lambda i: (i % tiles_per_seq, 0), seq=seq)))
    ps = dict(zip(names, _proj_call(xs, mod_s[1], mod_s[0], w_in_r, cos_s, sin_s, tm=ns_tok, mod_map=smap,
                                    pos_map=lambda i: (0, 0))))

    oa_p, ob_p = _attn_call(pp, lam4, subln, nbp, seq, lam_init, tq=tq)

    def rows(q, n_heads):
        return q.reshape(nbs, dseq, n_heads, HEAD_DIM).transpose(0, 2, 1, 3)

    qi_s = rows(ps["qi"], N_IDX_HEADS).reshape(nbs, N_IDX_HEADS * dseq, IDX_DIM)
    qa_s = _block_diag_rows(rows(ps["qa"], N_HEADS_A).reshape(nbs, N_KV_A, (N_HEADS_A // N_KV_A) * dseq, HEAD_DIM), N_KV_A)
    qb_r = rows(ps["qb"], 2 * N_HEADS_B)
    rb = N_HEADS_B // N_KV_B
    qb_r = qb_r.reshape(nbs, N_KV_B, rb, 2, dseq, HEAD_DIM)
    eye2 = jnp.eye(2, dtype=BF16)
    qb_s = jnp.einsum("bgrcqd,gh,ce->bgrcqhed", qb_r, eye2, eye2).reshape(nbs, N_KV_B * rb * 2 * dseq, N_KV_B * 2 * HEAD_DIM)

    def new_rows(a):
        return jnp.pad(a.reshape(nbs, dseq, -1), ((0, 0), (0, LANES - dseq), (0, 0)))

    oa_s, ob_s = _sample_call(
        page_table, qi_s, qa_s, qb_s, ps["kiwi"].reshape(nbs, dseq, LANES),
        new_rows(ps["kib"][:, :IDX_DIM]), new_rows(ps["kab"]), new_rows(ps["vab"]), new_rows(ps["kbb"]), new_rows(ps["vbb"]),
        lam4, subln,
        jnp.transpose(cache_k_idx[0], (0, 2, 1)),
        jnp.transpose(cache_k_a[0], (0, 2, 3, 1)).reshape(-1, KV_A, PAGE),
        jnp.transpose(cache_v_a[0], (0, 2, 3, 1)).reshape(-1, KV_A, PAGE),
        jnp.transpose(cache_k_b[0], (0, 2, 3, 4, 1)).reshape(-1, KV_B, PAGE),
        cache_v_b[0].reshape(-1, PAGE * N_KV_B, 2 * HEAD_DIM), lam_init, ppc=ppc)
    oa_s = oa_s.reshape(ns_tok, Q_A).astype(BF16)
    ob_s = ob_s.reshape(ns_tok, Q_B).astype(BF16)

    x1_p, h2_p, lg_p = _outproj_call(oa_p, ob_p, pp["sga"], pp["sgb"], xp, mod_p[2], mod_p[4], mod_p[3],
                                     wa, wb, wo, l1g, l1b, wrh, wrl, br, alpha, tm=tm, mod_map=pmap)
    x1_s, h2_s, lg_s = _outproj_call(oa_s, ob_s, ps["sga"], ps["sgb"], xs, mod_s[2], mod_s[4], mod_s[3],
                                     wa, wb, wo, l1g, l1b, wrh, wrl, br, alpha, tm=ns_tok, mod_map=smap)

    nblk_p = np_tok // tb
    xs_all, route_p, cnt_p = _dispatch_call(lg_p, h2_p, tb=tb, slots=slots, total_blocks=nblk_p + 1)
    xs_all, route_s, cnt_s = _dispatch_call(lg_s, h2_s, tb=ns_tok, slots=slots, total_blocks=nblk_p + 1,
                                            blk0=nblk_p, xs_buf=xs_all)
    cnt = jnp.concatenate([cnt_p.reshape(nblk_p, 8, LANES)[:, 0, :N_EXPERTS],
                           cnt_s.reshape(1, 8, LANES)[:, 0, :N_EXPERTS]], axis=0).astype(I32)
    total_units = (np_tok + ns_tok) * TOP_K_EXPERTS // UNIT + (nblk_p + 1) * N_EXPERTS
    max_tiles = total_units // TILE_UNITS + N_EXPERTS
    te, src, n_tiles = _unit_tables(cnt, upb, max_tiles)
    ys = _ffn_call(te, src, n_tiles, xs_all.reshape(-1, UNIT, D_MODEL), wg, bg, wu, bu, wd, bd).reshape(-1, D_MODEL)
    y_p = _combine_call(ys, route_p, x1_p, mod_p[5], l2g, l2b, alpha, tb=tb, slots=slots, blk0=0, mod_map=pmap)
    y_s = _combine_call(ys, route_s, x1_s, mod_s[5], l2g, l2b, alpha, tb=ns_tok, slots=slots, blk0=nblk_p, mod_map=smap)

    st = lambda a, nb_, t_, shp: a.reshape((1, nb_, t_) + shp)

    def st_t(a, shp):
        a = a.reshape((nbp,) + shp + (seq,))
        return jnp.transpose(a, (0, a.ndim - 1) + tuple(range(1, a.ndim - 1)))[None]

    outs = [y_p.reshape(nbp, seq, D_MODEL), y_s.reshape(nbs, dseq, D_MODEL),
            st_t(pp["ka"], (N_KV_A, HEAD_DIM)), st_t(pp["va"], (N_KV_A, HEAD_DIM)), st_t(pp["ki"], (IDX_DIM,)),
            st_t(pp["kb"], (N_KV_B, 2, HEAD_DIM)), st(pp["vb"], nbp, seq, (N_KV_B, 2 * HEAD_DIM)),
            st(ps["ka"], nbs, dseq, (N_KV_A, HEAD_DIM)), st(ps["va"], nbs, dseq, (N_KV_A, HEAD_DIM)),
            st(ps["ki"], nbs, dseq, (IDX_DIM,)), st(ps["kb"], nbs, dseq, (N_KV_B, 2, HEAD_DIM)),
            st(ps["vb"], nbs, dseq, (N_KV_B, 2 * HEAD_DIM))]
    return tuple(outs)
```

```python
import functools
import math

import jax
import jax.numpy as jnp
import numpy as np
from jax import lax
from jax.experimental import pallas as pl
from jax.experimental.pallas import tpu as pltpu

F32, BF16, I32 = jnp.float32, jnp.bfloat16, jnp.int32

D_MODEL = 1024
HEAD_DIM = 64
N_HEADS_A, N_KV_A = 8, 2
N_IDX_HEADS, IDX_DIM = 8, 64
TOPK_MAX = 256
N_HEADS_B, N_KV_B = 4, 2
N_EXPERTS, TOP_K_EXPERTS, D_FF = 32, 4, 1024
SWIGLU_ALPHA, SWIGLU_LIMIT = 1.702, 7.0
ROPE_THETA = 10000.0
LN_EPS = 1e-5
PAGE = 128
Q_A, KV_A = N_HEADS_A * HEAD_DIM, N_KV_A * HEAD_DIM
Q_IDX = N_IDX_HEADS * IDX_DIM
Q_B, KV_B = N_HEADS_B * 2 * HEAD_DIM, N_KV_B * 2 * HEAD_DIM
PROJ_SIZES = (Q_A, KV_A, KV_A, Q_IDX, IDX_DIM, N_IDX_HEADS, Q_B, KV_B, KV_B, D_MODEL, D_MODEL)
IDX_SCALE = (N_IDX_HEADS * IDX_DIM) ** -0.5
QK_SCALE = HEAD_DIM ** -0.5

LANES = 128
UNIT = 16
TILE_UNITS = 16
TILE_M = UNIT * TILE_UNITS
INT_MIN = -(2 ** 31)
NEG_INF = float("-inf")
VMEM_LIMIT = 56 * 1024 * 1024

C_QA, C_QI, C_QB = 0, 512, 1024
C_KA, C_VA, C_KB, C_VB, C_KIWI, C_GA, C_GB = 1536, 1664, 1792, 2048, 2304, 2432, 3456
W_IN_COLS = 4480


def _params(sem=None):
    return pltpu.CompilerParams(dimension_semantics=sem, vmem_limit_bytes=VMEM_LIMIT)


def _ln(xf):
    mu = jnp.mean(xf, axis=-1, keepdims=True)
    xc = xf - mu
    var = jnp.mean(xc * xc, axis=-1, keepdims=True)
    return xc * lax.rsqrt(var + LN_EPS)


def _dot(a, b):
    return jnp.dot(a, b, preferred_element_type=F32)


def _dot_nt(a, b):
    return lax.dot_general(a, b, (((1,), (1,)), ((), ())), preferred_element_type=F32)


def _ada_kernel(c_ref, w_ref, b_ref, o_ref):
    c = c_ref[...]
    s = c * jax.nn.sigmoid(c)
    o_ref[...] = jnp.dot(s, w_ref[...], preferred_element_type=F32,
                         precision=lax.Precision.HIGHEST) + b_ref[...]


def _ada_call(c, w, b):
    n = c.shape[0]
    tn = 1024
    return pl.pallas_call(
        _ada_kernel,
        out_shape=jax.ShapeDtypeStruct((n, 6 * D_MODEL), F32),
        grid=(6 * D_MODEL // tn,),
        in_specs=[pl.BlockSpec((n, D_MODEL), lambda j: (0, 0)),
                  pl.BlockSpec((D_MODEL, tn), lambda j: (0, j)),
                  pl.BlockSpec((1, tn), lambda j: (0, j))],
        out_specs=pl.BlockSpec((n, tn), lambda j: (0, j)),
        compiler_params=_params(("arbitrary",)),
        name="ada",
    )(c, w, b.reshape(1, -1))


_PROJ_OUTS = (
    ("qa", 512, BF16), ("qi", 512, BF16), ("qb", 512, BF16),
    ("ka", 128, F32), ("va", 128, F32), ("kb", 256, F32), ("vb", 256, F32), ("ki", 64, F32),
    ("kab", 128, BF16), ("vab", 128, BF16), ("kbb", 256, BF16), ("vbb", 256, BF16), ("kib", 128, BF16),
    ("kiwi", 128, F32), ("sga", 1024, BF16), ("sgb", 1024, BF16),
)


_STATE_T = ("ka", "va", "kb", "ki")


def _proj_kernel(state_t, x_ref, sc_ref, sh_ref, w_ref, cos_ref, sin_ref, *outs):
    o = dict(zip([n for n, _, _ in _PROJ_OUTS], outs))
    tm = x_ref.shape[0]

    def put_state(name, v, c0=0):
        wd = min(o[name].shape[-2 if state_t else -1] - c0, LANES)
        if state_t:
            o[name][0, c0:c0 + wd, :] = v.T[0:wd]
        else:
            o[name][:, c0:c0 + wd] = v[:, 0:wd]

    h = (_ln(x_ref[...]) * (1.0 + sc_ref[0]) + sh_ref[0]).astype(BF16)
    cos, sin = cos_ref[...], sin_ref[...]
    lane = lax.broadcasted_iota(I32, (tm, LANES), 1)
    lo_half = (lane % HEAD_DIM) < (HEAD_DIM // 2)

    def rope(v):
        sw = jnp.where(lo_half, pltpu.roll(v, LANES - HEAD_DIM // 2, 1), pltpu.roll(v, HEAD_DIM // 2, 1))
        return v * cos + sw * sin

    def proj(c0, width):
        return _dot(h, w_ref[:, c0:c0 + width])

    for name, c0 in (("qa", C_QA), ("qi", C_QI), ("qb", C_QB)):
        r = proj(c0, 512)
        for j in range(4):
            o[name][:, j * LANES:(j + 1) * LANES] = rope(r[:, j * LANES:(j + 1) * LANES]).astype(BF16)
    r = rope(proj(C_KA, 128))
    put_state("ka", r)
    o["kab"][...] = r.astype(BF16)
    r = proj(C_VA, 128)
    put_state("va", r)
    o["vab"][...] = r.astype(BF16)
    r = proj(C_KB, 256)
    for j in range(2):
        rr = rope(r[:, j * LANES:(j + 1) * LANES])
        put_state("kb", rr, j * LANES)
        o["kbb"][:, j * LANES:(j + 1) * LANES] = rr.astype(BF16)
    r = proj(C_VB, 256)
    o["vb"][...] = r
    o["vbb"][...] = r.astype(BF16)
    r = proj(C_KIWI, 128)
    rr = rope(r)
    put_state("ki", rr)
    o["kib"][...] = rr.astype(BF16)
    o["kiwi"][...] = jnp.where(lane < IDX_DIM, rr, r * IDX_SCALE)
    o["sga"][...] = jax.nn.sigmoid(proj(C_GA, D_MODEL)).astype(BF16)
    o["sgb"][...] = jax.nn.sigmoid(proj(C_GB, D_MODEL)).astype(BF16)


def _proj_call(x, sc, sh, w, cos, sin, *, tm, mod_map, pos_map, seq=None):
    n = x.shape[0]
    mod_rows = sc.shape[1]
    row = lambda i: (i, 0)
    out_shape, out_specs = [], []
    for name, wd, dt in _PROJ_OUTS:
        if seq is not None and name in _STATE_T:
            out_shape.append(jax.ShapeDtypeStruct((n // seq, wd, seq), dt))
            out_specs.append(pl.BlockSpec((1, wd, tm), lambda i: (i // (seq // tm), 0, i % (seq // tm))))
        else:
            out_shape.append(jax.ShapeDtypeStruct((n, wd), dt))
            out_specs.append(pl.BlockSpec((tm, wd), row))
    return pl.pallas_call(
        functools.partial(_proj_kernel, seq is not None),
        out_shape=out_shape,
        grid=(n // tm,),
        in_specs=[pl.BlockSpec((tm, D_MODEL), row),
                  pl.BlockSpec((1, mod_rows, D_MODEL), mod_map),
                  pl.BlockSpec((1, mod_rows, D_MODEL), mod_map),
                  pl.BlockSpec((D_MODEL, W_IN_COLS), lambda i: (0, 0)),
                  pl.BlockSpec((tm, LANES), pos_map),
                  pl.BlockSpec((tm, LANES), pos_map)],
        out_specs=out_specs,
        compiler_params=_params(("arbitrary",)),
        name="proj",
    )(x, sc, sh, w, cos, sin)


def _sortable_key(score):
    bits = pltpu.bitcast(score + 0.0, I32)
    return jnp.where(bits < 0, bits ^ jnp.int32(0x7FFFFFFF), bits)


def _kth_largest_key(count_ge, rows, k):
    t0 = jnp.where(count_ge(jnp.zeros((rows, 1), I32)) >= k, jnp.int32(0), jnp.int32(INT_MIN))

    def body(i, t):
        cand = t | jnp.left_shift(jnp.int32(1), 30 - i)
        return jnp.where(count_ge(cand) >= k, cand, t)

    return lax.fori_loop(0, 31, body, t0)


def _lambda_full(lam_ref, lam_init):
    l = lam_ref[...]
    a = jnp.exp(jnp.sum(l[0:1] * l[1:2], axis=-1, keepdims=True))
    b = jnp.exp(jnp.sum(l[2:3] * l[3:4], axis=-1, keepdims=True))
    return a - b + lam_init


def _attn_kernel(tq, topk, lam_init, qa_ref, qi_ref, qb_ref, kiwi_ref, kab_ref, vab_ref, kbb_ref, vbb_ref,
                 kib_ref, lam_ref, sub_ref, oa_ref, ob_ref,
                 key_ref, bias_ref, s_ref, qs_ref, qas_ref, qbs_ref, wb_ref, va1_ref, vb1_ref, m_ref,
                 acca_ref, accb_ref):
    i = pl.program_id(1)
    kc = tq
    n_ch = i + 1
    tiles = kc // LANES
    seq = kab_ref.shape[0]
    ra, rb = N_HEADS_A // N_KV_A, N_HEADS_B // N_KV_B
    row = lax.broadcasted_iota(I32, (tq, kc), 0)
    col = lax.broadcasted_iota(I32, (tq, kc), 1)

    def wide(x):
        return jnp.concatenate([x] * tiles, axis=1)

    def keys_of(c):
        return pl.ds(pl.multiple_of(c * kc, kc), kc)

    @pl.when(i == 0)
    def _():
        one = jnp.where(lax.broadcasted_iota(I32, (seq, HEAD_DIM), 1) == 0, 1.0, 0.0).astype(BF16)
        for g in range(N_KV_A):
            va1_ref[:, g * LANES:(g + 1) * LANES] = jnp.concatenate(
                [vab_ref[:, g * HEAD_DIM:(g + 1) * HEAD_DIM], one], axis=1)
        one = jnp.where(lax.broadcasted_iota(I32, (seq, LANES), 1) == 0, 1.0, 0.0).astype(BF16)
        for g in range(N_KV_B):
            vb1_ref[:, g * 2 * LANES:g * 2 * LANES + LANES] = vbb_ref[:, g * LANES:(g + 1) * LANES]
            vb1_ref[:, g * 2 * LANES + LANES:(g + 1) * 2 * LANES] = one

    kiw = kiwi_ref[...]
    for h in range(N_IDX_HEADS):
        qs_ref[h * tq:(h + 1) * tq, :] = qi_ref[:, h * IDX_DIM:(h + 1) * IDX_DIM]
        wb_ref[h] = jnp.broadcast_to(kiw[:, IDX_DIM + h:IDX_DIM + h + 1], (tq, LANES))

    def idx_chunk(c, carry):
        s = _dot_nt(qs_ref[...], kib_ref[keys_of(c), 0:IDX_DIM])
        score = jnp.zeros((tq, kc), F32)
        for h in range(N_IDX_HEADS):
            score = score + wide(wb_ref[h]) * jnp.maximum(s[h * tq:(h + 1) * tq], 0.0)
        adm = col <= row + (i - c) * kc
        key_ref[c] = jnp.where(adm, _sortable_key(score), jnp.int32(INT_MIN))
        return carry

    def walk(body):
        def pair(p, carry):
            body(2 * p, carry)
            body(2 * p + 1, carry)
            return carry

        lax.fori_loop(0, n_ch // 2, pair, 0)

        @pl.when(n_ch % 2 == 1)
        def _():
            body(n_ch - 1, 0)

    walk(idx_chunk)

    def count(pred, t, r0, nr):
        tb = jnp.broadcast_to(t, (nr, LANES))

        def body(c, acc):
            for n in range(tiles):
                acc = acc + jnp.where(pred(key_ref[c, r0:r0 + nr, n * LANES:(n + 1) * LANES], tb), 1.0, 0.0)
            return acc

        acc = lax.fori_loop(0, n_ch, body, jnp.zeros((nr, LANES), F32))
        return jnp.sum(acc, axis=-1, keepdims=True)

    ge = lambda k, t: k >= t
    hr = tq // 2
    thr = jnp.concatenate(
        [_kth_largest_key(lambda t, r0=r0: count(ge, t, r0, hr), hr, float(topk)) for r0 in (0, hr)], axis=0)
    n_ge = count(ge, thr, 0, tq)

    def bias_chunk(c, carry):
        k = key_ref[c]
        bias_ref[c] = jnp.where((k >= thr) & (k > INT_MIN), 0.0, NEG_INF)
        return carry

    lax.fori_loop(0, n_ch, bias_chunk, 0)
    tie = jnp.max(jnp.where((thr > INT_MIN) & (n_ge > float(topk)), 1.0, 0.0))

    @pl.when(tie > 0.0)
    def _():
        need = float(topk) - count(lambda k, t: k > t, thr, 0, tq)
        before = jnp.where(lax.broadcasted_iota(I32, (kc, kc), 0) < lax.broadcasted_iota(I32, (kc, kc), 1), 1.0, 0.0)

        def fix(c, seen):
            k = key_ref[c]
            eq = k == thr
            eqf = jnp.where(eq, 1.0, 0.0)
            prefix = _dot(eqf.astype(BF16), before.astype(BF16)) + seen
            sel = ((k > thr) | (eq & (prefix < need))) & (k > INT_MIN)
            bias_ref[c] = jnp.where(sel, 0.0, NEG_INF)
            return seen + jnp.sum(eqf, axis=-1, keepdims=True)

        lax.fori_loop(0, n_ch, fix, jnp.zeros((tq, 1), F32))

    def fold(x, op):
        r = x[:, 0:LANES]
        for t in range(1, tiles):
            r = op(r, x[:, t * LANES:(t + 1) * LANES])
        return r

    def mixer(stacks, bias_of, values, acc_ref):
        nr = stacks[0][0].shape[0]
        n_rows = nr * len(stacks)
        m_ref[0:n_rows, :] = jnp.full((n_rows, LANES), NEG_INF, F32)

        def pass1(c, carry):
            ks = keys_of(c)
            bias = jnp.concatenate([bias_of(c)] * (nr // tq), axis=0)
            for n, (q, k_ref, k_col) in enumerate(stacks):
                s = _dot_nt(q[...], k_ref[ks, k_col:k_col + HEAD_DIM]) + bias
                s_ref[c, n * nr:(n + 1) * nr, :] = s
                m_ref[n * nr:(n + 1) * nr, :] = jnp.maximum(m_ref[n * nr:(n + 1) * nr, :], fold(s, jnp.maximum))
            return carry

        walk(pass1)
        m_ref[0:n_rows, :] = jnp.broadcast_to(jnp.max(m_ref[0:n_rows, :], axis=-1, keepdims=True), (n_rows, LANES))
        acc_ref[...] = jnp.zeros_like(acc_ref)

        def pass2(c, carry):
            ks = keys_of(c)
            for n0, cnt, v_ref, v_col, vw in values:
                r0, r1 = n0 * nr, (n0 + cnt) * nr
                p = jnp.exp(s_ref[c, r0:r1, :] - wide(m_ref[r0:r1, :]))
                acc_ref[r0:r1, :] += _dot(p.astype(BF16), v_ref[ks, v_col:v_col + vw])
            return carry

        walk(pass2)

    for g in range(N_KV_A):
        for r in range(ra):
            qas_ref[g, r * tq:(r + 1) * tq, :] = qa_ref[:, (g * ra + r) * HEAD_DIM:(g * ra + r + 1) * HEAD_DIM]
    mixer([(qas_ref.at[g], kab_ref, g * HEAD_DIM) for g in range(N_KV_A)],
          lambda c: bias_ref[c],
          [(g, 1, va1_ref, g * LANES, LANES) for g in range(N_KV_A)], acca_ref)
    for g in range(N_KV_A):
        o = acca_ref[g * ra * tq:(g + 1) * ra * tq, :]
        o = o[:, 0:HEAD_DIM] / o[:, HEAD_DIM:HEAD_DIM + 1]
        oa_ref[:, g * ra * HEAD_DIM:(g + 1) * ra * HEAD_DIM] = jnp.concatenate(
            [o[r * tq:(r + 1) * tq] for r in range(ra)], axis=-1).astype(BF16)

    for g in range(N_KV_B):
        for mp in range(2):
            for r in range(rb):
                qc = ((g * rb + r) * 2 + mp) * HEAD_DIM
                qbs_ref[g * 2 + mp, r * tq:(r + 1) * tq, :] = qb_ref[:, qc:qc + HEAD_DIM]
    mixer([(qbs_ref.at[n], kbb_ref, n * HEAD_DIM) for n in range(2 * N_KV_B)],
          lambda c: jnp.where(col <= row + (i - c) * kc, 0.0, NEG_INF),
          [(2 * g, 2, vb1_ref, g * 2 * LANES, 2 * LANES) for g in range(N_KV_B)], accb_ref)
    lam = _lambda_full(lam_ref, lam_init)
    for g in range(N_KV_B):
        o = accb_ref[g * 2 * rb * tq:(g + 1) * 2 * rb * tq, :]
        o = o[:, 0:LANES] / o[:, LANES:LANES + 1]
        for r in range(rb):
            d = o[r * tq:(r + 1) * tq] - lam * o[(rb + r) * tq:(rb + r + 1) * tq]
            d = d * lax.rsqrt(jnp.mean(d * d, axis=-1, keepdims=True) + LN_EPS) * sub_ref[...] * (1.0 - lam_init)
            oc0 = (g * rb + r) * 2 * HEAD_DIM
            ob_ref[:, oc0:oc0 + 2 * HEAD_DIM] = d.astype(BF16)


def _attn_call(p, lam4, subln, n_batch, seq, lam_init, *, tq):
    nq = seq // tq
    topk = min(TOPK_MAX, seq // 4)
    n = n_batch * seq
    qmap = lambda b, i: (b * nq + i, 0)
    kmap = lambda b, i: (b, 0)
    const = lambda b, i: (0, 0)
    return pl.pallas_call(
        functools.partial(_attn_kernel, tq, topk, lam_init),
        out_shape=[jax.ShapeDtypeStruct((n, Q_A), BF16), jax.ShapeDtypeStruct((n, Q_B), BF16)],
        grid=(n_batch, nq),
        in_specs=[pl.BlockSpec((tq, 512), qmap), pl.BlockSpec((tq, 512), qmap), pl.BlockSpec((tq, 512), qmap),
                  pl.BlockSpec((tq, LANES), qmap),
                  pl.BlockSpec((seq, 128), kmap), pl.BlockSpec((seq, 128), kmap),
                  pl.BlockSpec((seq, 256), kmap), pl.BlockSpec((seq, 256), kmap),
                  pl.BlockSpec((seq, 128), kmap),
                  pl.BlockSpec((4, HEAD_DIM), const), pl.BlockSpec((1, 2 * HEAD_DIM), const)],
        out_specs=[pl.BlockSpec((tq, Q_A), qmap), pl.BlockSpec((tq, Q_B), qmap)],
        scratch_shapes=[pltpu.VMEM((nq, tq, tq), I32), pltpu.VMEM((nq, tq, tq), F32),
                        pltpu.VMEM((nq, N_HEADS_A * tq, tq), F32),
                        pltpu.VMEM((N_IDX_HEADS * tq, HEAD_DIM), BF16),
                        pltpu.VMEM((N_KV_A, (N_HEADS_A // N_KV_A) * tq, HEAD_DIM), BF16),
                        pltpu.VMEM((2 * N_KV_B, (N_HEADS_B // N_KV_B) * tq, HEAD_DIM), BF16),
                        pltpu.VMEM((N_IDX_HEADS, tq, LANES), F32),
                        pltpu.VMEM((seq, N_KV_A * LANES), BF16), pltpu.VMEM((seq, N_KV_B * 2 * LANES), BF16),
                        pltpu.VMEM((N_HEADS_A * tq, LANES), F32),
                        pltpu.VMEM((N_HEADS_A * tq, LANES), F32), pltpu.VMEM((2 * N_HEADS_B * tq, 2 * LANES), F32)],
        compiler_params=_params(("arbitrary", "arbitrary")),
        name="attn",
    )(p["qa"], p["qi"], p["qb"], p["kiwi"], p["kab"], p["vab"], p["kbb"], p["vbb"], p["kib"], lam4, subln)


def _sample_kernel(n_chunks, ppc, topk, lam_init,
                   pt_ref, qi_ref, qa_ref, qb_ref, wi_ref, kin_ref, kan_ref, van_ref, kbn_ref, vbn_ref,
                   lam_ref, sub_ref, cki_ref, cka_ref, cva_ref, ckb_ref, cvb_ref,
                   oa_ref, ob_ref,
                   pages, sem, key_ref, bias_ref, s_ref, acc_ref, l_ref):
    b = pl.program_id(0)
    ch = ppc * PAGE
    n_pages = n_chunks * ppc
    past = n_chunks * ch
    width = past + LANES
    nq = 8
    n_streams = 5
    burst = math.gcd(n_pages, 8)

    class Stream:
        def __init__(self, cache, order):
            self.cache, self.order, self.rows = cache, order, cache.shape[1]

        def slot(self, batch):
            return (batch * n_streams + self.order) % 2

        def copy(self, batch, page):
            s = self.slot(batch)
            return pltpu.make_async_copy(self.cache.at[pt_ref[batch, page]],
                                         pages.at[s, page, pl.ds(0, self.rows)], sem.at[s, page // ppc])

        def start(self, batch):
            def body(i, carry):
                for u in range(burst):
                    self.copy(batch, i * burst + u).start()
                return carry

            lax.fori_loop(0, n_pages // burst, body, 0)

        def run(self, fn, start_next):
            start_next()

            def body(c, carry):
                for pg in range(ppc):
                    self.copy(b, c * ppc + pg).wait()
                fn(c, pages.at[self.slot(b), pl.ds(c * ppc, ppc), pl.ds(0, self.rows)])
                return carry

            lax.fori_loop(0, n_chunks, body, 0)

    st_ki, st_ka, st_va = Stream(cki_ref, 0), Stream(cka_ref, 1), Stream(cva_ref, 2)
    st_kb, st_vb = Stream(ckb_ref, 3), Stream(cvb_ref, 4)

    @pl.when(b == 0)
    def _():
        st_ki.start(b)

    def start_next_batch():
        @pl.when(b + 1 < pl.num_programs(0))
        def _():
            st_ki.start(b + 1)

    def chunk_ds(c):
        return pl.ds(pl.multiple_of(c * ch, ch), ch)

    def pages_t(buf):
        return jnp.concatenate([buf[pg] for pg in range(ppc)], axis=1).astype(BF16)

    newcol = lax.broadcasted_iota(I32, (nq, LANES), 1)
    newrow = lax.broadcasted_iota(I32, (nq, LANES), 0)
    new_adm = newcol <= newrow

    wi = wi_ref[0]

    def head_sum(s):
        acc = jnp.zeros((nq, s.shape[1]), F32)
        for h in range(N_IDX_HEADS):
            acc = acc + wi[:, IDX_DIM + h:IDX_DIM + h + 1] * jnp.maximum(s[h * nq:(h + 1) * nq], 0.0)
        return acc

    def idx_fn(c, kbuf):
        s = _dot(qi_ref[0], pages_t(kbuf))
        key_ref[:, chunk_ds(c)] = _sortable_key(head_sum(s))

    st_ki.run(idx_fn, lambda: st_ka.start(b))
    s_new = head_sum(_dot_nt(qi_ref[0], kin_ref[0]))
    key_ref[:, past:width] = jnp.where(new_adm, _sortable_key(s_new), jnp.int32(INT_MIN))

    def count_ge(t):
        return jnp.sum(jnp.where(key_ref[...] >= t, 1.0, 0.0), axis=-1, keepdims=True)

    thr = _kth_largest_key(count_ge, nq, float(topk))
    key = key_ref[...]
    n_ge = count_ge(thr)
    bias_ref[...] = jnp.where((key >= thr) & (key > INT_MIN), 0.0, NEG_INF)
    tie = jnp.max(jnp.where((thr > INT_MIN) & (n_ge > float(topk)), 1.0, 0.0))

    @pl.when(tie > 0.0)
    def _():
        need = float(topk) - jnp.sum(jnp.where(key_ref[...] > thr, 1.0, 0.0), axis=-1, keepdims=True)

        def fix(lo, n, seen):
            k2 = key_ref[:, pl.ds(lo, n)]
            eq = k2 == thr
            before = lax.broadcasted_iota(I32, (n, n), 0) < lax.broadcasted_iota(I32, (n, n), 1)
            eqf = jnp.where(eq, 1.0, 0.0)
            prefix = _dot(eqf.astype(BF16), jnp.where(before, 1.0, 0.0).astype(BF16)) + seen
            sel = ((k2 > thr) | (eq & (prefix < need))) & (k2 > INT_MIN)
            bias_ref[:, pl.ds(lo, n)] = jnp.where(sel, 0.0, NEG_INF)
            return seen + jnp.sum(eqf, axis=-1, keepdims=True)

        seen = lax.fori_loop(0, n_chunks, lambda c, sn: fix(pl.multiple_of(c * ch, ch), ch, sn),
                             jnp.zeros((nq, 1), F32))
        fix(past, LANES, seen)

    def attend(q_ref, st_k, k_new_ref, st_v, after_v, add_values, add_new_values, bias_of):
        groups = q_ref.shape[1] // nq

        def tile_rows(x):
            return jnp.concatenate([x] * groups, axis=0)

        def k_fn(c, kb):
            s_ref[:, chunk_ds(c)] = _dot(q_ref[0], pages_t(kb)) + tile_rows(bias_of(chunk_ds(c), ch))

        st_k.run(k_fn, lambda: st_v.start(b))
        s_ref[:, past:width] = _dot_nt(q_ref[0], k_new_ref[0]) + tile_rows(bias_of(pl.ds(past, LANES), LANES))
        m = jnp.max(s_ref[...], axis=-1, keepdims=True)
        acc_ref[...] = jnp.zeros_like(acc_ref)
        l_ref[...] = jnp.zeros_like(l_ref)

        def v_fn(c, vb):
            p = jnp.exp(s_ref[:, chunk_ds(c)] - m)
            l_ref[...] += jnp.sum(p, axis=-1, keepdims=True)
            add_values(p.astype(BF16), vb)

        st_v.run(v_fn, after_v)
        p = jnp.exp(s_ref[:, past:width] - m)
        l_ref[...] += jnp.sum(p, axis=-1, keepdims=True)
        add_new_values(p.astype(BF16))
        return acc_ref[...] / l_ref[:, 0:1]

    def add_a(p, vb):
        acc_ref[...] += _dot_nt(p, pages_t(vb))

    def add_a_new(p):
        acc_ref[...] += _dot(p, van_ref[0])

    o = attend(qa_ref, st_ka, kan_ref, st_va, lambda: st_kb.start(b), add_a, add_a_new,
               lambda ds, n: bias_ref[:, ds])
    ra = N_HEADS_A // N_KV_A
    for g in range(N_KV_A):
        for r in range(ra):
            hh = g * ra + r
            oa_ref[0, :, hh * HEAD_DIM:(hh + 1) * HEAD_DIM] = o[hh * nq:(hh + 1) * nq, g * HEAD_DIM:(g + 1) * HEAD_DIM]

    cb_new = jnp.where(new_adm, 0.0, NEG_INF)
    rb = N_HEADS_B // N_KV_B
    grows = rb * 2 * nq

    def bias_b(ds, n):
        return cb_new if n == LANES else jnp.zeros((nq, n), F32)

    def add_b(p, vb):
        for g in range(N_KV_B):
            vg = jnp.concatenate([vb[pg, pl.ds(g, PAGE, stride=N_KV_B), :] for pg in range(ppc)], axis=0).astype(BF16)
            acc_ref[g * grows:(g + 1) * grows, :] += _dot(p[g * grows:(g + 1) * grows], vg)

    def add_b_new(p):
        for g in range(N_KV_B):
            acc_ref[g * grows:(g + 1) * grows, :] += _dot(p[g * grows:(g + 1) * grows],
                                                           vbn_ref[0, :, g * 2 * HEAD_DIM:(g + 1) * 2 * HEAD_DIM])

    o = attend(qb_ref, st_kb, kbn_ref, st_vb, start_next_batch, add_b, add_b_new, bias_b)
    lam = _lambda_full(lam_ref, lam_init)
    for g in range(N_KV_B):
        for r in range(rb):
            base = (g * rb + r) * 2 * nq
            cols = slice(0, 2 * HEAD_DIM)
            d = o[base:base + nq, cols] - lam * o[base + nq:base + 2 * nq, cols]
            d = d * lax.rsqrt(jnp.mean(d * d, axis=-1, keepdims=True) + LN_EPS) * sub_ref[...] * (1.0 - lam_init)
            oc0 = (g * rb + r) * 2 * HEAD_DIM
            ob_ref[0, :, oc0:oc0 + 2 * HEAD_DIM] = d


def _sample_call(page_table, qi, qa, qb, wi, kin, kan, van, kbn, vbn, lam4, subln,
                 cki, cka, cva, ckb, cvb, lam_init, *, ppc):
    nb, n_pages = page_table.shape
    n_chunks = n_pages // ppc
    assert n_chunks * ppc == n_pages
    ch = ppc * PAGE
    past = n_pages * PAGE
    width = past + LANES
    topk = min(TOPK_MAX, (past + 8) // 4)
    b3 = lambda b, pt: (b, 0, 0)
    const = lambda b, pt: (0, 0)
    any_spec = pl.BlockSpec(memory_space=pl.ANY)
    full = lambda a: pl.BlockSpec((1,) + a.shape[1:], b3)
    grid_spec = pltpu.PrefetchScalarGridSpec(
        num_scalar_prefetch=1,
        grid=(nb,),
        in_specs=[full(qi), full(qa), full(qb), full(wi), full(kin), full(kan), full(van), full(kbn), full(vbn),
                  pl.BlockSpec((4, HEAD_DIM), const), pl.BlockSpec((1, 2 * HEAD_DIM), const),
                  any_spec, any_spec, any_spec, any_spec, any_spec],
        out_specs=[pl.BlockSpec((1, 8, Q_A), b3), pl.BlockSpec((1, 8, Q_B), b3)],
        scratch_shapes=[pltpu.VMEM((2, n_pages, KV_B, PAGE), F32),
                        pltpu.SemaphoreType.DMA((2, n_chunks)),
                        pltpu.VMEM((8, width), I32), pltpu.VMEM((8, width), F32),
                        pltpu.VMEM((64, width), F32), pltpu.VMEM((64, LANES), F32), pltpu.VMEM((64, LANES), F32)],
    )
    return pl.pallas_call(
        functools.partial(_sample_kernel, n_chunks, ppc, topk, lam_init),
        out_shape=[jax.ShapeDtypeStruct((nb, 8, Q_A), F32), jax.ShapeDtypeStruct((nb, 8, Q_B), F32)],
        grid_spec=grid_spec,
        compiler_params=_params(("arbitrary",)),
        name="sample_mix",
    )(page_table, qi, qa, qb, wi, kin, kan, van, kbn, vbn, lam4, subln, cki, cka, cva, ckb, cvb)


def _split_bf16(x):
    hi = x.astype(BF16)
    return hi, (x - hi.astype(F32)).astype(BF16)


def _outproj_kernel(alpha, oa_ref, ob_ref, sga_ref, sgb_ref, x_ref, g1_ref, sc2_ref, sh2_ref,
                    wa_ref, wb_ref, wo_ref, lg_ref, lb_ref, wrh_ref, wrl_ref, br_ref,
                    x1_ref, h2_ref, lo_ref):
    merged = (sga_ref[...].astype(F32) * _dot(oa_ref[...], wa_ref[...])
              + sgb_ref[...].astype(F32) * _dot(ob_ref[...], wb_ref[...]))
    y = _dot(merged.astype(BF16), wo_ref[...])
    x1 = _ln(alpha * x_ref[...] + (1.0 + g1_ref[0]) * y) * lg_ref[...] + lb_ref[...]
    x1_ref[...] = x1
    h2 = _ln(x1) * (1.0 + sc2_ref[0]) + sh2_ref[0]
    h2_ref[...] = h2.astype(BF16)
    hi, lo = _split_bf16(h2)
    logits = _dot(hi, wrh_ref[...]) + (_dot(lo, wrh_ref[...]) + _dot(hi, wrl_ref[...])) + br_ref[...]
    lane = lax.broadcasted_iota(I32, logits.shape, 1)
    lo_ref[...] = jnp.where(lane < N_EXPERTS, logits, NEG_INF)


def _outproj_call(oa, ob, sga, sgb, x, g1, sc2, sh2, wa, wb, wo, lg, lb, wrh, wrl, br, alpha, *, tm, mod_map):
    n = x.shape[0]
    mod_rows = g1.shape[1]
    row = lambda i: (i, 0)
    const = lambda i: (0, 0)
    mod = pl.BlockSpec((1, mod_rows, D_MODEL), mod_map)
    return pl.pallas_call(
        functools.partial(_outproj_kernel, alpha),
        out_shape=[jax.ShapeDtypeStruct((n, D_MODEL), F32), jax.ShapeDtypeStruct((n, D_MODEL), BF16),
                   jax.ShapeDtypeStruct((n, LANES), F32)],
        grid=(n // tm,),
        in_specs=[pl.BlockSpec((tm, 512), row), pl.BlockSpec((tm, 512), row),
                  pl.BlockSpec((tm, D_MODEL), row), pl.BlockSpec((tm, D_MODEL), row),
                  pl.BlockSpec((tm, D_MODEL), row), mod, mod, mod,
                  pl.BlockSpec((512, D_MODEL), const), pl.BlockSpec((512, D_MODEL), const),
                  pl.BlockSpec((D_MODEL, D_MODEL), const),
                  pl.BlockSpec((1, D_MODEL), const), pl.BlockSpec((1, D_MODEL), const),
                  pl.BlockSpec((D_MODEL, LANES), const), pl.BlockSpec((D_MODEL, LANES), const),
                  pl.BlockSpec((1, LANES), const)],
        out_specs=[pl.BlockSpec((tm, D_MODEL), row), pl.BlockSpec((tm, D_MODEL), row), pl.BlockSpec((tm, LANES), row)],
        compiler_params=_params(("arbitrary",)),
        name="outproj",
    )(oa, ob, sga, sgb, x, g1, sc2, sh2, wa, wb, wo, lg, lb, wrh, wrl, br)


def _dispatch_kernel(slots, n_real, lo_ref, h2_ref, *rest):
    xs_ref, route_ref, cnt_ref = rest[-3:]

    @pl.when(pl.program_id(0) < n_real)
    def _():
        _dispatch_block(slots, lo_ref, h2_ref, xs_ref, route_ref, cnt_ref)

    @pl.when(pl.program_id(0) >= n_real)
    def _():
        xs_ref[...] = jnp.zeros_like(xs_ref)


def _dispatch_block(slots, lo_ref, h2_ref, xs_ref, route_ref, cnt_ref):
    tb = lo_ref.shape[0]
    lane = lax.broadcasted_iota(I32, (tb, LANES), 1)
    l = lo_ref[...]
    vals, hots = [], []
    for _ in range(TOP_K_EXPERTS):
        m = jnp.max(l, axis=-1, keepdims=True)
        idx = jnp.min(jnp.where(l == m, lane, LANES), axis=-1, keepdims=True)
        hot = lane == idx
        vals.append(m)
        hots.append(hot)
        l = jnp.where(hot, NEG_INF, l)
    es = [jnp.exp(v - vals[0]) for v in vals]
    den = es[0] + es[1] + es[2] + es[3]
    ws = [e / den for e in es]
    sel = jnp.zeros((tb, LANES), F32)
    for hot in hots:
        sel = sel + jnp.where(hot, 1.0, 0.0)
    earlier = lax.broadcasted_iota(I32, (tb, tb), 1) < lax.broadcasted_iota(I32, (tb, tb), 0)
    pos = _dot(jnp.where(earlier, 1.0, 0.0).astype(BF16), sel.astype(BF16))
    cnt = jnp.sum(sel, axis=0, keepdims=True)
    run_units = jnp.right_shift(cnt.astype(I32) + (UNIT - 1), 4).astype(F32)
    lower = lax.broadcasted_iota(I32, (LANES, LANES), 0) < lax.broadcasted_iota(I32, (LANES, LANES), 1)
    off_units = _dot(jnp.broadcast_to(run_units, (8, LANES)).astype(BF16), jnp.where(lower, 1.0, 0.0).astype(BF16))
    dest = off_units[0:1] * float(UNIT) + pos
    route = jnp.zeros((tb, LANES), F32)
    for j in range(TOP_K_EXPERTS):
        dj = jnp.sum(jnp.where(hots[j], dest, 0.0), axis=-1, keepdims=True)
        route = route + jnp.where(lane == j, dj, 0.0) + jnp.where(lane == TOP_K_EXPERTS + j, ws[j], 0.0)
    route_ref[...] = route
    cnt_ref[...] = jnp.broadcast_to(cnt, (8, LANES))
    route_t = route.T
    slot = lax.broadcasted_iota(I32, (slots, tb), 0).astype(F32)
    hit = slot == route_t[0:1]
    for j in range(1, TOP_K_EXPERTS):
        hit = hit | (slot == route_t[j:j + 1])
    xs_ref[...] = _dot(jnp.where(hit, 1.0, 0.0).astype(BF16), h2_ref[...]).astype(BF16)


def _dispatch_call(logits, h2, *, tb, slots, total_blocks, blk0=0, xs_buf=None):
    n = h2.shape[0]
    nb = n // tb
    steps = nb if xs_buf is not None else total_blocks - blk0
    row = lambda i: (jnp.minimum(i, nb - 1), 0)
    args, in_specs, aliases = [logits, h2], [pl.BlockSpec((tb, LANES), row), pl.BlockSpec((tb, D_MODEL), row)], {}
    if xs_buf is not None:
        args.append(xs_buf)
        in_specs.append(pl.BlockSpec(memory_space=pl.ANY))
        aliases = {2: 0}
    return pl.pallas_call(
        functools.partial(_dispatch_kernel, slots, nb),
        out_shape=[jax.ShapeDtypeStruct((total_blocks * slots, D_MODEL), BF16), jax.ShapeDtypeStruct((n, LANES), F32),
                   jax.ShapeDtypeStruct((nb * 8, LANES), F32)],
        grid=(steps,),
        in_specs=in_specs,
        out_specs=[pl.BlockSpec((slots, D_MODEL), lambda i: (blk0 + i, 0)), pl.BlockSpec((tb, LANES), row),
                   pl.BlockSpec((8, LANES), row)],
        input_output_aliases=aliases,
        compiler_params=_params(("arbitrary",)),
        name="dispatch",
    )(*args)


def _ffn_kernel(te_ref, src_ref, nt_ref, xs_ref, wg_ref, bg_ref, wu_ref, bu_ref, wd_ref, bd_ref, ys_ref,
                xbuf, ybuf, wgb, wub, wdb, insem, outsem):
    j = pl.program_id(0)
    nt = nt_ref[0]
    slot = j % 2

    @pl.when((j == 0) | (te_ref[j] != te_ref[jnp.maximum(j - 1, 0)]))
    def _():
        wgb[...] = wg_ref[0].astype(BF16)
        wub[...] = wu_ref[0].astype(BF16)
        wdb[...] = wd_ref[0].astype(BF16)

    def in_copy(tile, s, k):
        u = jnp.maximum(src_ref[tile * TILE_UNITS + k], 0)
        return pltpu.make_async_copy(xs_ref.at[u], xbuf.at[s, pl.ds(k * UNIT, UNIT)], insem.at[s])

    def out_copy(tile, s, k):
        u = jnp.maximum(src_ref[tile * TILE_UNITS + k], 0)
        return pltpu.make_async_copy(ybuf.at[s, pl.ds(k * UNIT, UNIT)], ys_ref.at[u], outsem.at[s])

    def for_valid_units(tile, fn):
        for k in range(TILE_UNITS):
            @pl.when(src_ref[tile * TILE_UNITS + k] >= 0)
            def _(k=k):
                fn(k)

    def start_in(tile, s):
        for_valid_units(tile, lambda k: in_copy(tile, s, k).start())

    def wait_out(tile, s):
        for_valid_units(tile, lambda k: out_copy(tile, s, k).wait())

    @pl.when(j == 0)
    def _():
        xbuf[...] = jnp.zeros_like(xbuf)

        @pl.when(nt > 0)
        def _():
            start_in(0, 0)

    @pl.when(j + 1 < nt)
    def _():
        start_in(j + 1, 1 - slot)

    @pl.when(j < nt)
    def _():
        for_valid_units(j, lambda k: in_copy(j, slot, k).wait())
        x = xbuf[slot]
        gate = jnp.minimum(_dot(x, wgb[...]) + bg_ref[0], SWIGLU_LIMIT)
        up = jnp.clip(_dot(x, wub[...]) + bu_ref[0], -SWIGLU_LIMIT, SWIGLU_LIMIT)
        act = (up + 1.0) * gate * jax.nn.sigmoid(SWIGLU_ALPHA * gate)
        y = _dot(act.astype(BF16), wdb[...]) + bd_ref[0]

        @pl.when(j >= 2)
        def _():
            wait_out(j - 2, slot)

        ybuf[slot] = y.astype(BF16)
        for_valid_units(j, lambda k: out_copy(j, slot, k).start())

    @pl.when(j == pl.num_programs(0) - 1)
    def _():
        @pl.when(nt >= 2)
        def _():
            wait_out(nt - 2, nt % 2)

        @pl.when(nt >= 1)
        def _():
            wait_out(nt - 1, (nt - 1) % 2)


def _ffn_call(tile_expert, src_units, n_tiles, xs_units, wg, bg, wu, bu, wd, bd):
    n_units = xs_units.shape[0]
    max_tiles = tile_expert.shape[0]
    wmap = lambda j, te, src, nt: (te[j], 0, 0)
    any_spec = pl.BlockSpec(memory_space=pl.ANY)
    grid_spec = pltpu.PrefetchScalarGridSpec(
        num_scalar_prefetch=3,
        grid=(max_tiles,),
        in_specs=[any_spec,
                  pl.BlockSpec((1, D_MODEL, D_FF), wmap), pl.BlockSpec((1, 1, D_FF), wmap),
                  pl.BlockSpec((1, D_MODEL, D_FF), wmap), pl.BlockSpec((1, 1, D_FF), wmap),
                  pl.BlockSpec((1, D_FF, D_MODEL), wmap), pl.BlockSpec((1, 1, D_MODEL), wmap)],
        out_specs=any_spec,
        scratch_shapes=[pltpu.VMEM((2, TILE_M, D_MODEL), BF16), pltpu.VMEM((2, TILE_M, D_MODEL), BF16),
                        pltpu.VMEM((D_MODEL, D_FF), BF16), pltpu.VMEM((D_MODEL, D_FF), BF16),
                        pltpu.VMEM((D_FF, D_MODEL), BF16),
                        pltpu.SemaphoreType.DMA((2,)), pltpu.SemaphoreType.DMA((2,))],
    )
    return pl.pallas_call(
        _ffn_kernel,
        out_shape=jax.ShapeDtypeStruct((n_units, UNIT, D_MODEL), BF16),
        grid_spec=grid_spec,
        input_output_aliases={3: 0},
        compiler_params=_params(("arbitrary",)),
        name="ffn",
    )(tile_expert, src_units, n_tiles, xs_units, wg, bg, wu, bu, wd, bd)


def _unit_tables(cnt, units_per_block, max_tiles):
    nb = cnt.shape[0]
    run_units = (cnt + (UNIT - 1)) // UNIT
    blk_off = jnp.cumsum(run_units, axis=1) - run_units
    cum = jnp.cumsum(run_units, axis=0)
    tot = cum[-1]
    tiles = (tot + (TILE_UNITS - 1)) // TILE_UNITS
    tile_end = jnp.cumsum(tiles)
    n_tiles = tile_end[-1]
    j = jnp.arange(max_tiles, dtype=I32)
    jc = jnp.minimum(j, n_tiles - 1)
    te = jnp.minimum(jnp.sum(tile_end[None, :] <= jc[:, None], axis=1), N_EXPERTS - 1).astype(I32)
    tile_start = (tile_end - tiles)[te]
    rank = ((j - tile_start) * TILE_UNITS)[:, None] + jnp.arange(TILE_UNITS, dtype=I32)[None, :]
    cum_t = cum.T[te]
    blk = jnp.sum(cum_t[:, None, :] <= rank[:, :, None], axis=-1).astype(I32)
    blk_c = jnp.minimum(blk, nb - 1)
    before = jnp.take_along_axis(cum_t - run_units.T[te], blk_c, axis=1)
    unit = blk_c * units_per_block + blk_off[blk_c, te[:, None]] + (rank - before)
    valid = (rank < tot[te][:, None]) & (j < n_tiles)[:, None]
    src = jnp.where(valid, unit, -1).astype(I32).reshape(-1)
    return te, src, n_tiles.astype(I32).reshape(1)


def _combine_kernel(alpha, ys_ref, route_ref, x1_ref, g2_ref, lg_ref, lb_ref, y_ref):
    tb = x1_ref.shape[0]
    slots = ys_ref.shape[0]
    route = route_ref[...]
    slot = lax.broadcasted_iota(I32, (tb, slots), 1).astype(F32)
    pw = jnp.zeros((tb, slots), F32)
    for j in range(TOP_K_EXPERTS):
        pw = pw + jnp.where(slot == route[:, j:j + 1], route[:, TOP_K_EXPERTS + j:TOP_K_EXPERTS + j + 1], 0.0)
    moe = _dot(pw.astype(BF16), ys_ref[...])
    y_ref[...] = _ln(alpha * x1_ref[...] + (1.0 + g2_ref[0]) * moe) * lg_ref[...] + lb_ref[...]


def _combine_call(ys, route, x1, g2, lg, lb, alpha, *, tb, slots, blk0, mod_map):
    n = x1.shape[0]
    mod_rows = g2.shape[1]
    row = lambda i: (i, 0)
    const = lambda i: (0, 0)
    return pl.pallas_call(
        functools.partial(_combine_kernel, alpha),
        out_shape=jax.ShapeDtypeStruct((n, D_MODEL), F32),
        grid=(n // tb,),
        in_specs=[pl.BlockSpec((slots, D_MODEL), lambda i: (blk0 + i, 0)),
                  pl.BlockSpec((tb, LANES), row), pl.BlockSpec((tb, D_MODEL), row),
                  pl.BlockSpec((1, mod_rows, D_MODEL), mod_map),
                  pl.BlockSpec((1, D_MODEL), const), pl.BlockSpec((1, D_MODEL), const)],
        out_specs=pl.BlockSpec((tb, D_MODEL), row),
        compiler_params=_params(("arbitrary",)),
        name="combine",
    )(ys, route, x1, g2, lg, lb)


def _rope_tables(pos):
    half = HEAD_DIM // 2
    inv = ROPE_THETA ** (-jnp.arange(half, dtype=F32) / half)
    ang = jnp.asarray(pos).astype(F32)[:, None] * inv[None, :]
    cos, sin = jnp.cos(ang), jnp.sin(ang)
    return (jnp.tile(cos, (1, LANES // half)),
            jnp.tile(jnp.concatenate([-sin, sin], axis=1), (1, LANES // HEAD_DIM)))


def _prep_w_in(w):
    offs = np.concatenate([[0], np.cumsum(PROJ_SIZES)])
    qa, ka, va, qi, ki, wi, qb, kb, vb, ga, gb = [w[:, offs[i]:offs[i + 1]] for i in range(11)]
    pad = jnp.zeros((D_MODEL, LANES - IDX_DIM - N_IDX_HEADS), w.dtype)
    return jnp.concatenate([qa * QK_SCALE, qi, qb * QK_SCALE, ka, va, kb, vb, ki, wi, pad, ga, gb], axis=1).astype(BF16)


def _block_diag_rows(q, n_blocks):
    eye = jnp.eye(n_blocks, dtype=q.dtype)
    nb, _, rows, d = q.shape
    return jnp.einsum("bgrd,gh->bgrhd", q, eye).reshape(nb, n_blocks * rows, n_blocks * d)


def kernel(x_prompt, x_sample, cache_k_a, cache_v_a, cache_k_idx, cache_k_b, cache_v_b, page_table, c_prompt, c_sample, w_ada, b_ada, w_in, w_br_a, w_br_b, w_o, lambda_q1, lambda_k1, lambda_q2, lambda_k2, subln_g, ln1_g, ln1_b, ln2_g, ln2_b, w_router, b_router, w_gate, b_gate, w_up, b_up, w_down, b_down):
    nbp, seq, _ = x_prompt.shape
    nbs, dseq, _ = x_sample.shape
    depth = w_ada.shape[0]
    assert depth == 1 and dseq == 8
    n_pages = page_table.shape[1]
    past = n_pages * PAGE
    alpha = (2 * depth) ** 0.25
    lam_init = 0.8 - 0.6 * math.exp(-0.3 * 0)
    np_tok, ns_tok = nbp * seq, nbs * dseq

    tm = min(512, seq)
    tq = min(256, seq)
    tb = tm
    slots = ((4 * tb + N_EXPERTS * (UNIT - 1) + TILE_M - 1) // TILE_M) * TILE_M
    upb = slots // UNIT
    ppc = min(16, n_pages // 2)

    w_in_r = _prep_w_in(w_in[0])
    wa, wb, wo = w_br_a[0].astype(BF16), w_br_b[0].astype(BF16), w_o[0].astype(BF16)
    wr = jnp.pad(w_router[0], ((0, 0), (0, LANES - N_EXPERTS)))
    wrh = wr.astype(BF16)
    wrl = (wr - wrh.astype(F32)).astype(BF16)
    br = jnp.pad(b_router[0], (0, LANES - N_EXPERTS)).reshape(1, LANES)
    wg, wu, wd = w_gate[0], w_up[0], w_down[0]
    bg, bu, bd = b_gate[0][:, None, :], b_up[0][:, None, :], b_down[0][:, None, :]
    lam4 = jnp.stack([lambda_q1[0], lambda_k1[0], lambda_q2[0], lambda_k2[0]])
    subln = subln_g[0].reshape(1, -1)
    l1g, l1b, l2g, l2b = (a[0].reshape(1, -1) for a in (ln1_g, ln1_b, ln2_g, ln2_b))

    mod = _ada_call(jnp.concatenate([c_prompt, c_sample], axis=0), w_ada[0], b_ada[0])
    mod_p = [m.reshape(nbp, 1, D_MODEL) for m in jnp.split(mod[:nbp], 6, axis=-1)]
    mod_s = [jnp.repeat(m, dseq, axis=0).reshape(1, ns_tok, D_MODEL) for m in jnp.split(mod[nbp:], 6, axis=-1)]
    tiles_per_seq = seq // tm
    pmap = lambda i: (i // tiles_per_seq, 0, 0)
    smap = lambda i: (0, 0, 0)

    cos_p, sin_p = _rope_tables(np.arange(seq))
    cos_s, sin_s = _rope_tables(np.tile(past + np.arange(dseq), nbs))
    xp = x_prompt.reshape(np_tok, D_MODEL)
    xs = x_sample.reshape(ns_tok, D_MODEL)
    names = [n for n, _, _ in _PROJ_OUTS]
    pp = dict(zip(names, _proj_call(xp, mod_p[1], mod_p[0], w_in_r, cos_p, sin_p, tm=tm, mod_map=pmap,
                                    pos_map=lambda i: (i % tiles_per_seq, 0), seq=seq)))
    ps = dict(zip(names, _proj_call(xs, mod_s[1], mod_s[0], w_in_r, cos_s, sin_s, tm=ns_tok, mod_map=smap,
                                    pos_map=lambda i: (0, 0))))

    oa_p, ob_p = _attn_call(pp, lam4, subln, nbp, seq, lam_init, tq=tq)

    def rows(q, n_heads):
        return q.reshape(nbs, dseq, n_heads, HEAD_DIM).transpose(0, 2, 1, 3)

    qi_s = rows(ps["qi"], N_IDX_HEADS).reshape(nbs, N_IDX_HEADS * dseq, IDX_DIM)
    qa_s = _block_diag_rows(rows(ps["qa"], N_HEADS_A).reshape(nbs, N_KV_A, (N_HEADS_A // N_KV_A) * dseq, HEAD_DIM), N_KV_A)
    qb_r = rows(ps["qb"], 2 * N_HEADS_B)
    rb = N_HEADS_B // N_KV_B
    qb_r = qb_r.reshape(nbs, N_KV_B, rb, 2, dseq, HEAD_DIM)
    eye2 = jnp.eye(2, dtype=BF16)
    qb_s = jnp.einsum("bgrcqd,gh,ce->bgrcqhed", qb_r, eye2, eye2).reshape(nbs, N_KV_B * rb * 2 * dseq, N_KV_B * 2 * HEAD_DIM)

    def new_rows(a):
        return jnp.pad(a.reshape(nbs, dseq, -1), ((0, 0), (0, LANES - dseq), (0, 0)))

    oa_s, ob_s = _sample_call(
        page_table, qi_s, qa_s, qb_s, ps["kiwi"].reshape(nbs, dseq, LANES),
        new_rows(ps["kib"][:, :IDX_DIM]), new_rows(ps["kab"]), new_rows(ps["vab"]), new_rows(ps["kbb"]), new_rows(ps["vbb"]),
        lam4, subln,
        jnp.transpose(cache_k_idx[0], (0, 2, 1)),
        jnp.transpose(cache_k_a[0], (0, 2, 3, 1)).reshape(-1, KV_A, PAGE),
        jnp.transpose(cache_v_a[0], (0, 2, 3, 1)).reshape(-1, KV_A, PAGE),
        jnp.transpose(cache_k_b[0], (0, 2, 3, 4, 1)).reshape(-1, KV_B, PAGE),
        cache_v_b[0].reshape(-1, PAGE * N_KV_B, 2 * HEAD_DIM), lam_init, ppc=ppc)
    oa_s = oa_s.reshape(ns_tok, Q_A).astype(BF16)
    ob_s = ob_s.reshape(ns_tok, Q_B).astype(BF16)

    x1_p, h2_p, lg_p = _outproj_call(oa_p, ob_p, pp["sga"], pp["sgb"], xp, mod_p[2], mod_p[4], mod_p[3],
                                     wa, wb, wo, l1g, l1b, wrh, wrl, br, alpha, tm=tm, mod_map=pmap)
    x1_s, h2_s, lg_s = _outproj_call(oa_s, ob_s, ps["sga"], ps["sgb"], xs, mod_s[2], mod_s[4], mod_s[3],
                                     wa, wb, wo, l1g, l1b, wrh, wrl, br, alpha, tm=ns_tok, mod_map=smap)

    nblk_p = np_tok // tb
    xs_all, route_p, cnt_p = _dispatch_call(lg_p, h2_p, tb=tb, slots=slots, total_blocks=nblk_p + 1)
    xs_all, route_s, cnt_s = _dispatch_call(lg_s, h2_s, tb=ns_tok, slots=slots, total_blocks=nblk_p + 1,
                                            blk0=nblk_p, xs_buf=xs_all)
    cnt = jnp.concatenate([cnt_p.reshape(nblk_p, 8, LANES)[:, 0, :N_EXPERTS],
                           cnt_s.reshape(1, 8, LANES)[:, 0, :N_EXPERTS]], axis=0).astype(I32)
    total_units = (np_tok + ns_tok) * TOP_K_EXPERTS // UNIT + (nblk_p + 1) * N_EXPERTS
    max_tiles = total_units // TILE_UNITS + N_EXPERTS
    te, src, n_tiles = _unit_tables(cnt, upb, max_tiles)
    ys = _ffn_call(te, src, n_tiles, xs_all.reshape(-1, UNIT, D_MODEL), wg, bg, wu, bu, wd, bd).reshape(-1, D_MODEL)
    y_p = _combine_call(ys, route_p, x1_p, mod_p[5], l2g, l2b, alpha, tb=tb, slots=slots, blk0=0, mod_map=pmap)
    y_s = _combine_call(ys, route_s, x1_s, mod_s[5], l2g, l2b, alpha, tb=ns_tok, slots=slots, blk0=nblk_p, mod_map=smap)

    st = lambda a, nb_, t_, shp: a.reshape((1, nb_, t_) + shp)

    def st_t(a, shp):
        a = a.reshape((nbp,) + shp + (seq,))
        return jnp.transpose(a, (0, a.ndim - 1) + tuple(range(1, a.ndim - 1)))[None]

    outs = [y_p.reshape(nbp, seq, D_MODEL), y_s.reshape(nbs, dseq, D_MODEL),
            st_t(pp["ka"], (N_KV_A, HEAD_DIM)), st_t(pp["va"], (N_KV_A, HEAD_DIM)), st_t(pp["ki"], (IDX_DIM,)),
            st_t(pp["kb"], (N_KV_B, 2, HEAD_DIM)), st(pp["vb"], nbp, seq, (N_KV_B, 2 * HEAD_DIM)),
            st(ps["ka"], nbs, dseq, (N_KV_A, HEAD_DIM)), st(ps["va"], nbs, dseq, (N_KV_A, HEAD_DIM)),
            st(ps["ki"], nbs, dseq, (IDX_DIM,)), st(ps["kb"], nbs, dseq, (N_KV_B, 2, HEAD_DIM)),
            st(ps["vb"], nbs, dseq, (N_KV_B, 2 * HEAD_DIM))]
    return tuple(outs)
```

```python
import functools
import math

import jax
import jax.numpy as jnp
import numpy as np
from jax import lax
from jax.experimental import pallas as pl
from jax.experimental.pallas import tpu as pltpu

F32, BF16, I32, I16 = jnp.float32, jnp.bfloat16, jnp.int32, jnp.int16

D_MODEL = 1024
HEAD_DIM = 64
N_HEADS_A, N_KV_A = 8, 2
N_IDX_HEADS, IDX_DIM = 8, 64
TOPK_MAX = 256
N_HEADS_B, N_KV_B = 4, 2
N_EXPERTS, TOP_K_EXPERTS, D_FF = 32, 4, 1024
SWIGLU_ALPHA, SWIGLU_LIMIT = 1.702, 7.0
ROPE_THETA = 10000.0
LN_EPS = 1e-5
PAGE = 128
Q_A, KV_A = N_HEADS_A * HEAD_DIM, N_KV_A * HEAD_DIM
Q_IDX = N_IDX_HEADS * IDX_DIM
Q_B, KV_B = N_HEADS_B * 2 * HEAD_DIM, N_KV_B * 2 * HEAD_DIM
PROJ_SIZES = (Q_A, KV_A, KV_A, Q_IDX, IDX_DIM, N_IDX_HEADS, Q_B, KV_B, KV_B, D_MODEL, D_MODEL)
IDX_SCALE = (N_IDX_HEADS * IDX_DIM) ** -0.5
QK_SCALE = HEAD_DIM ** -0.5

LANES = 128
UNIT = 16
TILE_UNITS = 16
TILE_M = UNIT * TILE_UNITS
INT_MIN = -(2 ** 31)
NEG_INF = float("-inf")
VMEM_LIMIT = 56 * 1024 * 1024

C_QA, C_QI, C_QB = 0, 512, 1024
C_KA, C_VA, C_KB, C_VB, C_KIWI, C_GA, C_GB = 1536, 1664, 1792, 2048, 2304, 2432, 3456
W_IN_COLS = 4480


def _params(sem=None):
    return pltpu.CompilerParams(dimension_semantics=sem, vmem_limit_bytes=VMEM_LIMIT)


def _ln(xf):
    mu = jnp.mean(xf, axis=-1, keepdims=True)
    xc = xf - mu
    var = jnp.mean(xc * xc, axis=-1, keepdims=True)
    return xc * lax.rsqrt(var + LN_EPS)


def _dot(a, b):
    return jnp.dot(a, b, preferred_element_type=F32)


def _dot_nt(a, b):
    return lax.dot_general(a, b, (((1,), (1,)), ((), ())), preferred_element_type=F32)


def _ada_kernel(c_ref, w_ref, b_ref, o_ref):
    c = c_ref[...]
    s = c * jax.nn.sigmoid(c)
    o_ref[...] = jnp.dot(s, w_ref[...], preferred_element_type=F32,
                         precision=lax.Precision.HIGHEST) + b_ref[...]


def _ada_call(c, w, b):
    n = c.shape[0]
    tn = 1024
    return pl.pallas_call(
        _ada_kernel,
        out_shape=jax.ShapeDtypeStruct((n, 6 * D_MODEL), F32),
        grid=(6 * D_MODEL // tn,),
        in_specs=[pl.BlockSpec((n, D_MODEL), lambda j: (0, 0)),
                  pl.BlockSpec((D_MODEL, tn), lambda j: (0, j)),
                  pl.BlockSpec((1, tn), lambda j: (0, j))],
        out_specs=pl.BlockSpec((n, tn), lambda j: (0, j)),
        compiler_params=_params(("arbitrary",)),
        name="ada",
    )(c, w, b.reshape(1, -1))


_PROJ_OUTS = (
    ("qa", 512, BF16), ("qi", 512, BF16), ("qb", 512, BF16),
    ("ka", 128, F32), ("va", 128, F32), ("kb", 256, F32), ("vb", 256, F32), ("ki", 64, F32),
    ("kab", 128, BF16), ("vab", 128, BF16), ("kbb", 256, BF16), ("vbb", 256, BF16), ("kib", 128, BF16),
    ("kiwi", 128, F32), ("sga", 1024, BF16), ("sgb", 1024, BF16),
)


_STATE_T = ("ka", "va", "kb", "ki")


def _proj_kernel(state_t, x_ref, sc_ref, sh_ref, w_ref, cos_ref, sin_ref, *outs):
    o = dict(zip([n for n, _, _ in _PROJ_OUTS], outs))
    tm = x_ref.shape[0]

    def put_state(name, v, c0=0):
        wd = min(o[name].shape[-2 if state_t else -1] - c0, LANES)
        if state_t:
            o[name][0, c0:c0 + wd, :] = v.T[0:wd]
        else:
            o[name][:, c0:c0 + wd] = v[:, 0:wd]

    h = (_ln(x_ref[...]) * (1.0 + sc_ref[0]) + sh_ref[0]).astype(BF16)
    cos, sin = cos_ref[...], sin_ref[...]
    lane = lax.broadcasted_iota(I32, (tm, LANES), 1)
    lo_half = (lane % HEAD_DIM) < (HEAD_DIM // 2)

    def rope(v):
        sw = jnp.where(lo_half, pltpu.roll(v, LANES - HEAD_DIM // 2, 1), pltpu.roll(v, HEAD_DIM // 2, 1))
        return v * cos + sw * sin

    def proj(c0, width):
        return _dot(h, w_ref[:, c0:c0 + width])

    for name, c0 in (("qa", C_QA), ("qi", C_QI), ("qb", C_QB)):
        r = proj(c0, 512)
        for j in range(4):
            o[name][:, j * LANES:(j + 1) * LANES] = rope(r[:, j * LANES:(j + 1) * LANES]).astype(BF16)
    r = rope(proj(C_KA, 128))
    put_state("ka", r)
    o["kab"][...] = r.astype(BF16)
    r = proj(C_VA, 128)
    put_state("va", r)
    o["vab"][...] = r.astype(BF16)
    r = proj(C_KB, 256)
    for j in range(2):
        rr = rope(r[:, j * LANES:(j + 1) * LANES])
        put_state("kb", rr, j * LANES)
        o["kbb"][:, j * LANES:(j + 1) * LANES] = rr.astype(BF16)
    r = proj(C_VB, 256)
    o["vb"][...] = r
    o["vbb"][...] = r.astype(BF16)
    r = proj(C_KIWI, 128)
    rr = rope(r)
    put_state("ki", rr)
    o["kib"][...] = rr.astype(BF16)
    o["kiwi"][...] = jnp.where(lane < IDX_DIM, rr, r * IDX_SCALE)
    o["sga"][...] = jax.nn.sigmoid(proj(C_GA, D_MODEL)).astype(BF16)
    o["sgb"][...] = jax.nn.sigmoid(proj(C_GB, D_MODEL)).astype(BF16)


def _proj_call(x, sc, sh, w, cos, sin, *, tm, mod_map, pos_map, seq=None):
    n = x.shape[0]
    mod_rows = sc.shape[1]
    row = lambda i: (i, 0)
    out_shape, out_specs = [], []
    for name, wd, dt in _PROJ_OUTS:
        if seq is not None and name in _STATE_T:
            out_shape.append(jax.ShapeDtypeStruct((n // seq, wd, seq), dt))
            out_specs.append(pl.BlockSpec((1, wd, tm), lambda i: (i // (seq // tm), 0, i % (seq // tm))))
        else:
            out_shape.append(jax.ShapeDtypeStruct((n, wd), dt))
            out_specs.append(pl.BlockSpec((tm, wd), row))
    return pl.pallas_call(
        functools.partial(_proj_kernel, seq is not None),
        out_shape=out_shape,
        grid=(n // tm,),
        in_specs=[pl.BlockSpec((tm, D_MODEL), row),
                  pl.BlockSpec((1, mod_rows, D_MODEL), mod_map),
                  pl.BlockSpec((1, mod_rows, D_MODEL), mod_map),
                  pl.BlockSpec((D_MODEL, W_IN_COLS), lambda i: (0, 0)),
                  pl.BlockSpec((tm, LANES), pos_map),
                  pl.BlockSpec((tm, LANES), pos_map)],
        out_specs=out_specs,
        compiler_params=_params(("arbitrary",)),
        name="proj",
    )(x, sc, sh, w, cos, sin)


def _sortable_key(score):
    bits = pltpu.bitcast(score + 0.0, I32)
    return jnp.where(bits < 0, bits ^ jnp.int32(0x7FFFFFFF), bits)


def _kth_largest_key(count_ge, rows, k):
    t0 = jnp.where(count_ge(jnp.zeros((rows, 1), I32)) >= k, jnp.int32(0), jnp.int32(INT_MIN))

    def body(i, t):
        cand = t | jnp.left_shift(jnp.int32(1), 30 - i)
        return jnp.where(count_ge(cand) >= k, cand, t)

    return lax.fori_loop(0, 31, body, t0)


def _kth_largest_half(count_ge, rows, k):
    t0 = jnp.where(count_ge(jnp.zeros((rows, 1), I32)) >= k, jnp.int32(0), jnp.int32(-32768))

    def body(i, t):
        cand = t | jnp.left_shift(jnp.int32(1), 14 - i)
        return jnp.where(count_ge(cand) >= k, cand, t)

    return lax.fori_loop(0, 15, body, t0)


def _lambda_full(lam_ref, lam_init):
    l = lam_ref[...]
    a = jnp.exp(jnp.sum(l[0:1] * l[1:2], axis=-1, keepdims=True))
    b = jnp.exp(jnp.sum(l[2:3] * l[3:4], axis=-1, keepdims=True))
    return a - b + lam_init


def _attn_kernel(tq, topk, lam_init, qa_ref, qi_ref, qb_ref, kiwi_ref, kab_ref, vab_ref, kbb_ref, vbb_ref,
                 kib_ref, lam_ref, sub_ref, oa_ref, ob_ref,
                 key_ref, hi_ref, lo_ref, bias_ref, s_ref, qs_ref, qas_ref, qbs_ref, wb_ref, va1_ref, vb1_ref, m_ref,
                 acca_ref, accb_ref):
    i = pl.program_id(1)
    kc = tq
    n_ch = i + 1
    tiles = kc // LANES
    seq = kab_ref.shape[0]
    ra, rb = N_HEADS_A // N_KV_A, N_HEADS_B // N_KV_B
    row = lax.broadcasted_iota(I32, (tq, kc), 0)
    col = lax.broadcasted_iota(I32, (tq, kc), 1)

    def wide(x):
        return jnp.concatenate([x] * tiles, axis=1)

    def keys_of(c):
        return pl.ds(pl.multiple_of(c * kc, kc), kc)

    @pl.when(i == 0)
    def _():
        one = jnp.where(lax.broadcasted_iota(I32, (seq, HEAD_DIM), 1) == 0, 1.0, 0.0).astype(BF16)
        for g in range(N_KV_A):
            va1_ref[:, g * LANES:(g + 1) * LANES] = jnp.concatenate(
                [vab_ref[:, g * HEAD_DIM:(g + 1) * HEAD_DIM], one], axis=1)
        one = jnp.where(lax.broadcasted_iota(I32, (seq, LANES), 1) == 0, 1.0, 0.0).astype(BF16)
        for g in range(N_KV_B):
            vb1_ref[:, g * 2 * LANES:g * 2 * LANES + LANES] = vbb_ref[:, g * LANES:(g + 1) * LANES]
            vb1_ref[:, g * 2 * LANES + LANES:(g + 1) * 2 * LANES] = one

    kiw = kiwi_ref[...]
    for h in range(N_IDX_HEADS):
        qs_ref[h * tq:(h + 1) * tq, :] = qi_ref[:, h * IDX_DIM:(h + 1) * IDX_DIM]
        wb_ref[h] = jnp.broadcast_to(kiw[:, IDX_DIM + h:IDX_DIM + h + 1], (tq, LANES))

    def idx_chunk(c, carry):
        s = _dot_nt(qs_ref[...], kib_ref[keys_of(c), 0:IDX_DIM])
        score = jnp.zeros((tq, kc), F32)
        for h in range(N_IDX_HEADS):
            score = score + wide(wb_ref[h]) * jnp.maximum(s[h * tq:(h + 1) * tq], 0.0)
        adm = col <= row + (i - c) * kc
        key = jnp.where(adm, _sortable_key(score), jnp.int32(INT_MIN))
        key_ref[c] = key
        hi_ref[c] = (key >> 16).astype(I16)
        lo_ref[c] = ((key & 0xFFFF) - 32768).astype(I16)
        return carry

    def walk(body):
        def pair(p, carry):
            body(2 * p, carry)
            body(2 * p + 1, carry)
            return carry

        lax.fori_loop(0, n_ch // 2, pair, 0)

        @pl.when(n_ch % 2 == 1)
        def _():
            body(n_ch - 1, 0)

    walk(idx_chunk)

    def count(pred, t, r0, nr):
        tb = jnp.broadcast_to(t, (nr, LANES))

        def body(c, acc):
            for n in range(tiles):
                acc = acc + jnp.where(pred(key_ref[c, r0:r0 + nr, n * LANES:(n + 1) * LANES], tb), 1.0, 0.0)
            return acc

        acc = lax.fori_loop(0, n_ch, body, jnp.zeros((nr, LANES), F32))
        return jnp.sum(acc, axis=-1, keepdims=True)

    def count16(ref, t, r0, nr, strict=False):
        tb = jnp.broadcast_to(t, (nr, LANES)).astype(I16)

        def body(c, acc):
            for n in range(tiles):
                v = ref[c, r0:r0 + nr, n * LANES:(n + 1) * LANES]
                acc = acc + jnp.where((v > tb) if strict else (v >= tb), jnp.int16(1), jnp.int16(0))
            return acc

        acc = lax.fori_loop(0, n_ch, body, jnp.zeros((nr, LANES), I16))
        return jnp.sum(acc.astype(F32), axis=-1, keepdims=True)

    ge = lambda k, t: k >= t
    hr = tq // 2
    halves = (0, hr)
    t_hi = jnp.concatenate(
        [_kth_largest_half(lambda t, r0=r0: count16(hi_ref, t, r0, hr), hr, float(topk)) for r0 in halves], axis=0)
    need_lo = float(topk) - count16(hi_ref, t_hi, 0, tq, strict=True)
    t_hi16 = jnp.broadcast_to(t_hi, (tq, LANES)).astype(I16)

    def restrict(c, carry):
        lo_ref[c] = jnp.where(hi_ref[c] == wide(t_hi16), lo_ref[c], jnp.int16(-32768))
        return carry

    lax.fori_loop(0, n_ch, restrict, 0)
    t_lo = jnp.concatenate(
        [_kth_largest_half(lambda t, r0=r0: count16(lo_ref, t, r0, hr), hr, need_lo[r0:r0 + hr]) for r0 in halves],
        axis=0)
    thr = t_hi * 65536 + (t_lo + 32768)
    n_ge = count(ge, thr, 0, tq)

    def bias_chunk(c, carry):
        k = key_ref[c]
        bias_ref[c] = jnp.where((k >= thr) & (k > INT_MIN), 0.0, NEG_INF)
        return carry

    lax.fori_loop(0, n_ch, bias_chunk, 0)
    tie = jnp.max(jnp.where((thr > INT_MIN) & (n_ge > float(topk)), 1.0, 0.0))

    @pl.when(tie > 0.0)
    def _():
        need = float(topk) - count(lambda k, t: k > t, thr, 0, tq)
        before = jnp.where(lax.broadcasted_iota(I32, (kc, kc), 0) < lax.broadcasted_iota(I32, (kc, kc), 1), 1.0, 0.0)

        def fix(c, seen):
            k = key_ref[c]
            eq = k == thr
            eqf = jnp.where(eq, 1.0, 0.0)
            prefix = _dot(eqf.astype(BF16), before.astype(BF16)) + seen
            sel = ((k > thr) | (eq & (prefix < need))) & (k > INT_MIN)
            bias_ref[c] = jnp.where(sel, 0.0, NEG_INF)
            return seen + jnp.sum(eqf, axis=-1, keepdims=True)

        lax.fori_loop(0, n_ch, fix, jnp.zeros((tq, 1), F32))

    def fold(x, op):
        r = x[:, 0:LANES]
        for t in range(1, tiles):
            r = op(r, x[:, t * LANES:(t + 1) * LANES])
        return r

    def mixer(stacks, bias_of, values, acc_ref):
        nr = stacks[0][0].shape[0]
        n_rows = nr * len(stacks)
        m_ref[0:n_rows, :] = jnp.full((n_rows, LANES), NEG_INF, F32)

        def pass1(c, carry):
            ks = keys_of(c)
            bias = jnp.concatenate([bias_of(c)] * (nr // tq), axis=0)
            for n, (q, k_ref, k_col) in enumerate(stacks):
                s = _dot_nt(q[...], k_ref[ks, k_col:k_col + HEAD_DIM]) + bias
                s_ref[c, n * nr:(n + 1) * nr, :] = s
                m_ref[n * nr:(n + 1) * nr, :] = jnp.maximum(m_ref[n * nr:(n + 1) * nr, :], fold(s, jnp.maximum))
            return carry

        walk(pass1)
        m_ref[0:n_rows, :] = jnp.broadcast_to(jnp.max(m_ref[0:n_rows, :], axis=-1, keepdims=True), (n_rows, LANES))
        acc_ref[...] = jnp.zeros_like(acc_ref)

        def pass2(c, carry):
            ks = keys_of(c)
            for n0, cnt, v_ref, v_col, vw in values:
                r0, r1 = n0 * nr, (n0 + cnt) * nr
                p = jnp.exp(s_ref[c, r0:r1, :] - wide(m_ref[r0:r1, :]))
                acc_ref[r0:r1, :] += _dot(p.astype(BF16), v_ref[ks, v_col:v_col + vw])
            return carry

        walk(pass2)

    for g in range(N_KV_A):
        for r in range(ra):
            qas_ref[g, r * tq:(r + 1) * tq, :] = qa_ref[:, (g * ra + r) * HEAD_DIM:(g * ra + r + 1) * HEAD_DIM]
    mixer([(qas_ref.at[g], kab_ref, g * HEAD_DIM) for g in range(N_KV_A)],
          lambda c: bias_ref[c],
          [(g, 1, va1_ref, g * LANES, LANES) for g in range(N_KV_A)], acca_ref)
    for g in range(N_KV_A):
        o = acca_ref[g * ra * tq:(g + 1) * ra * tq, :]
        o = o[:, 0:HEAD_DIM] / o[:, HEAD_DIM:HEAD_DIM + 1]
        oa_ref[:, g * ra * HEAD_DIM:(g + 1) * ra * HEAD_DIM] = jnp.concatenate(
            [o[r * tq:(r + 1) * tq] for r in range(ra)], axis=-1).astype(BF16)

    for g in range(N_KV_B):
        for mp in range(2):
            for r in range(rb):
                qc = ((g * rb + r) * 2 + mp) * HEAD_DIM
                qbs_ref[g * 2 + mp, r * tq:(r + 1) * tq, :] = qb_ref[:, qc:qc + HEAD_DIM]
    mixer([(qbs_ref.at[n], kbb_ref, n * HEAD_DIM) for n in range(2 * N_KV_B)],
          lambda c: jnp.where(col <= row + (i - c) * kc, 0.0, NEG_INF),
          [(2 * g, 2, vb1_ref, g * 2 * LANES, 2 * LANES) for g in range(N_KV_B)], accb_ref)
    lam = _lambda_full(lam_ref, lam_init)
    for g in range(N_KV_B):
        o = accb_ref[g * 2 * rb * tq:(g + 1) * 2 * rb * tq, :]
        o = o[:, 0:LANES] / o[:, LANES:LANES + 1]
        for r in range(rb):
            d = o[r * tq:(r + 1) * tq] - lam * o[(rb + r) * tq:(rb + r + 1) * tq]
            d = d * lax.rsqrt(jnp.mean(d * d, axis=-1, keepdims=True) + LN_EPS) * sub_ref[...] * (1.0 - lam_init)
            oc0 = (g * rb + r) * 2 * HEAD_DIM
            ob_ref[:, oc0:oc0 + 2 * HEAD_DIM] = d.astype(BF16)


def _attn_call(p, lam4, subln, n_batch, seq, lam_init, *, tq):
    nq = seq // tq
    topk = min(TOPK_MAX, seq // 4)
    n = n_batch * seq
    qmap = lambda b, i: (b * nq + i, 0)
    kmap = lambda b, i: (b, 0)
    const = lambda b, i: (0, 0)
    return pl.pallas_call(
        functools.partial(_attn_kernel, tq, topk, lam_init),
        out_shape=[jax.ShapeDtypeStruct((n, Q_A), BF16), jax.ShapeDtypeStruct((n, Q_B), BF16)],
        grid=(n_batch, nq),
        in_specs=[pl.BlockSpec((tq, 512), qmap), pl.BlockSpec((tq, 512), qmap), pl.BlockSpec((tq, 512), qmap),
                  pl.BlockSpec((tq, LANES), qmap),
                  pl.BlockSpec((seq, 128), kmap), pl.BlockSpec((seq, 128), kmap),
                  pl.BlockSpec((seq, 256), kmap), pl.BlockSpec((seq, 256), kmap),
                  pl.BlockSpec((seq, 128), kmap),
                  pl.BlockSpec((4, HEAD_DIM), const), pl.BlockSpec((1, 2 * HEAD_DIM), const)],
        out_specs=[pl.BlockSpec((tq, Q_A), qmap), pl.BlockSpec((tq, Q_B), qmap)],
        scratch_shapes=[pltpu.VMEM((nq, tq, tq), I32), pltpu.VMEM((nq, tq, tq), I16), pltpu.VMEM((nq, tq, tq), I16),
                        pltpu.VMEM((nq, tq, tq), F32),
                        pltpu.VMEM((nq, N_HEADS_A * tq, tq), F32),
                        pltpu.VMEM((N_IDX_HEADS * tq, HEAD_DIM), BF16),
                        pltpu.VMEM((N_KV_A, (N_HEADS_A // N_KV_A) * tq, HEAD_DIM), BF16),
                        pltpu.VMEM((2 * N_KV_B, (N_HEADS_B // N_KV_B) * tq, HEAD_DIM), BF16),
                        pltpu.VMEM((N_IDX_HEADS, tq, LANES), F32),
                        pltpu.VMEM((seq, N_KV_A * LANES), BF16), pltpu.VMEM((seq, N_KV_B * 2 * LANES), BF16),
                        pltpu.VMEM((N_HEADS_A * tq, LANES), F32),
                        pltpu.VMEM((N_HEADS_A * tq, LANES), F32), pltpu.VMEM((2 * N_HEADS_B * tq, 2 * LANES), F32)],
        compiler_params=_params(("arbitrary", "arbitrary")),
        name="attn",
    )(p["qa"], p["qi"], p["qb"], p["kiwi"], p["kab"], p["vab"], p["kbb"], p["vbb"], p["kib"], lam4, subln)


def _sample_kernel(n_chunks, ppc, topk, lam_init,
                   pt_ref, qi_ref, qa_ref, qb_ref, wi_ref, kin_ref, kan_ref, van_ref, kbn_ref, vbn_ref,
                   lam_ref, sub_ref, cki_ref, cka_ref, cva_ref, ckb_ref, cvb_ref,
                   oa_ref, ob_ref,
                   pages, sem, key_ref, bias_ref, s_ref, acc_ref, l_ref):
    b = pl.program_id(0)
    ch = ppc * PAGE
    n_pages = n_chunks * ppc
    past = n_chunks * ch
    width = past + LANES
    nq = 8
    n_streams = 5
    burst = math.gcd(n_pages, 8)

    class Stream:
        def __init__(self, cache, order):
            self.cache, self.order, self.rows = cache, order, cache.shape[1]

        def slot(self, batch):
            return (batch * n_streams + self.order) % 2

        def copy(self, batch, page):
            s = self.slot(batch)
            return pltpu.make_async_copy(self.cache.at[pt_ref[batch, page]],
                                         pages.at[s, page, pl.ds(0, self.rows)], sem.at[s, page // ppc])

        def start(self, batch):
            def body(i, carry):
                for u in range(burst):
                    self.copy(batch, i * burst + u).start()
                return carry

            lax.fori_loop(0, n_pages // burst, body, 0)

        def run(self, fn, start_next):
            start_next()

            def body(c, carry):
                for pg in range(ppc):
                    self.copy(b, c * ppc + pg).wait()
                fn(c, pages.at[self.slot(b), pl.ds(c * ppc, ppc), pl.ds(0, self.rows)])
                return carry

            lax.fori_loop(0, n_chunks, body, 0)

    st_ki, st_ka, st_va = Stream(cki_ref, 0), Stream(cka_ref, 1), Stream(cva_ref, 2)
    st_kb, st_vb = Stream(ckb_ref, 3), Stream(cvb_ref, 4)

    @pl.when(b == 0)
    def _():
        st_ki.start(b)

    def start_next_batch():
        @pl.when(b + 1 < pl.num_programs(0))
        def _():
            st_ki.start(b + 1)

    def chunk_ds(c):
        return pl.ds(pl.multiple_of(c * ch, ch), ch)

    def pages_t(buf):
        return jnp.concatenate([buf[pg] for pg in range(ppc)], axis=1).astype(BF16)

    newcol = lax.broadcasted_iota(I32, (nq, LANES), 1)
    newrow = lax.broadcasted_iota(I32, (nq, LANES), 0)
    new_adm = newcol <= newrow

    wi = wi_ref[0]

    def head_sum(s):
        acc = jnp.zeros((nq, s.shape[1]), F32)
        for h in range(N_IDX_HEADS):
            acc = acc + wi[:, IDX_DIM + h:IDX_DIM + h + 1] * jnp.maximum(s[h * nq:(h + 1) * nq], 0.0)
        return acc

    def idx_fn(c, kbuf):
        s = _dot(qi_ref[0], pages_t(kbuf))
        key_ref[:, chunk_ds(c)] = _sortable_key(head_sum(s))

    st_ki.run(idx_fn, lambda: st_ka.start(b))
    s_new = head_sum(_dot_nt(qi_ref[0], kin_ref[0]))
    key_ref[:, past:width] = jnp.where(new_adm, _sortable_key(s_new), jnp.int32(INT_MIN))

    def count_ge(t):
        return jnp.sum(jnp.where(key_ref[...] >= t, 1.0, 0.0), axis=-1, keepdims=True)

    thr = _kth_largest_key(count_ge, nq, float(topk))
    key = key_ref[...]
    n_ge = count_ge(thr)
    bias_ref[...] = jnp.where((key >= thr) & (key > INT_MIN), 0.0, NEG_INF)
    tie = jnp.max(jnp.where((thr > INT_MIN) & (n_ge > float(topk)), 1.0, 0.0))

    @pl.when(tie > 0.0)
    def _():
        need = float(topk) - jnp.sum(jnp.where(key_ref[...] > thr, 1.0, 0.0), axis=-1, keepdims=True)

        def fix(lo, n, seen):
            k2 = key_ref[:, pl.ds(lo, n)]
            eq = k2 == thr
            before = lax.broadcasted_iota(I32, (n, n), 0) < lax.broadcasted_iota(I32, (n, n), 1)
            eqf = jnp.where(eq, 1.0, 0.0)
            prefix = _dot(eqf.astype(BF16), jnp.where(before, 1.0, 0.0).astype(BF16)) + seen
            sel = ((k2 > thr) | (eq & (prefix < need))) & (k2 > INT_MIN)
            bias_ref[:, pl.ds(lo, n)] = jnp.where(sel, 0.0, NEG_INF)
            return seen + jnp.sum(eqf, axis=-1, keepdims=True)

        seen = lax.fori_loop(0, n_chunks, lambda c, sn: fix(pl.multiple_of(c * ch, ch), ch, sn),
                             jnp.zeros((nq, 1), F32))
        fix(past, LANES, seen)

    def attend(q_ref, st_k, k_new_ref, st_v, after_v, add_values, add_new_values, bias_of):
        groups = q_ref.shape[1] // nq

        def tile_rows(x):
            return jnp.concatenate([x] * groups, axis=0)

        def k_fn(c, kb):
            s_ref[:, chunk_ds(c)] = _dot(q_ref[0], pages_t(kb)) + tile_rows(bias_of(chunk_ds(c), ch))

        st_k.run(k_fn, lambda: st_v.start(b))
        s_ref[:, past:width] = _dot_nt(q_ref[0], k_new_ref[0]) + tile_rows(bias_of(pl.ds(past, LANES), LANES))
        m = jnp.max(s_ref[...], axis=-1, keepdims=True)
        acc_ref[...] = jnp.zeros_like(acc_ref)
        l_ref[...] = jnp.zeros_like(l_ref)

        def v_fn(c, vb):
            p = jnp.exp(s_ref[:, chunk_ds(c)] - m)
            l_ref[...] += jnp.sum(p, axis=-1, keepdims=True)
            add_values(p.astype(BF16), vb)

        st_v.run(v_fn, after_v)
        p = jnp.exp(s_ref[:, past:width] - m)
        l_ref[...] += jnp.sum(p, axis=-1, keepdims=True)
        add_new_values(p.astype(BF16))
        return acc_ref[...] / l_ref[:, 0:1]

    def add_a(p, vb):
        acc_ref[...] += _dot_nt(p, pages_t(vb))

    def add_a_new(p):
        acc_ref[...] += _dot(p, van_ref[0])

    o = attend(qa_ref, st_ka, kan_ref, st_va, lambda: st_kb.start(b), add_a, add_a_new,
               lambda ds, n: bias_ref[:, ds])
    ra = N_HEADS_A // N_KV_A
    for g in range(N_KV_A):
        for r in range(ra):
            hh = g * ra + r
            oa_ref[0, :, hh * HEAD_DIM:(hh + 1) * HEAD_DIM] = o[hh * nq:(hh + 1) * nq, g * HEAD_DIM:(g + 1) * HEAD_DIM]

    cb_new = jnp.where(new_adm, 0.0, NEG_INF)
    rb = N_HEADS_B // N_KV_B
    grows = rb * 2 * nq

    def bias_b(ds, n):
        return cb_new if n == LANES else jnp.zeros((nq, n), F32)

    def add_b(p, vb):
        for g in range(N_KV_B):
            vg = jnp.concatenate([vb[pg, pl.ds(g, PAGE, stride=N_KV_B), :] for pg in range(ppc)], axis=0).astype(BF16)
            acc_ref[g * grows:(g + 1) * grows, :] += _dot(p[g * grows:(g + 1) * grows], vg)

    def add_b_new(p):
        for g in range(N_KV_B):
            acc_ref[g * grows:(g + 1) * grows, :] += _dot(p[g * grows:(g + 1) * grows],
                                                           vbn_ref[0, :, g * 2 * HEAD_DIM:(g + 1) * 2 * HEAD_DIM])

    o = attend(qb_ref, st_kb, kbn_ref, st_vb, start_next_batch, add_b, add_b_new, bias_b)
    lam = _lambda_full(lam_ref, lam_init)
    for g in range(N_KV_B):
        for r in range(rb):
            base = (g * rb + r) * 2 * nq
            cols = slice(0, 2 * HEAD_DIM)
            d = o[base:base + nq, cols] - lam * o[base + nq:base + 2 * nq, cols]
            d = d * lax.rsqrt(jnp.mean(d * d, axis=-1, keepdims=True) + LN_EPS) * sub_ref[...] * (1.0 - lam_init)
            oc0 = (g * rb + r) * 2 * HEAD_DIM
            ob_ref[0, :, oc0:oc0 + 2 * HEAD_DIM] = d


def _sample_call(page_table, qi, qa, qb, wi, kin, kan, van, kbn, vbn, lam4, subln,
                 cki, cka, cva, ckb, cvb, lam_init, *, ppc):
    nb, n_pages = page_table.shape
    n_chunks = n_pages // ppc
    assert n_chunks * ppc == n_pages
    ch = ppc * PAGE
    past = n_pages * PAGE
    width = past + LANES
    topk = min(TOPK_MAX, (past + 8) // 4)
    b3 = lambda b, pt: (b, 0, 0)
    const = lambda b, pt: (0, 0)
    any_spec = pl.BlockSpec(memory_space=pl.ANY)
    full = lambda a: pl.BlockSpec((1,) + a.shape[1:], b3)
    grid_spec = pltpu.PrefetchScalarGridSpec(
        num_scalar_prefetch=1,
        grid=(nb,),
        in_specs=[full(qi), full(qa), full(qb), full(wi), full(kin), full(kan), full(van), full(kbn), full(vbn),
                  pl.BlockSpec((4, HEAD_DIM), const), pl.BlockSpec((1, 2 * HEAD_DIM), const),
                  any_spec, any_spec, any_spec, any_spec, any_spec],
        out_specs=[pl.BlockSpec((1, 8, Q_A), b3), pl.BlockSpec((1, 8, Q_B), b3)],
        scratch_shapes=[pltpu.VMEM((2, n_pages, KV_B, PAGE), F32),
                        pltpu.SemaphoreType.DMA((2, n_chunks)),
                        pltpu.VMEM((8, width), I32), pltpu.VMEM((8, width), F32),
                        pltpu.VMEM((64, width), F32), pltpu.VMEM((64, LANES), F32), pltpu.VMEM((64, LANES), F32)],
    )
    return pl.pallas_call(
        functools.partial(_sample_kernel, n_chunks, ppc, topk, lam_init),
        out_shape=[jax.ShapeDtypeStruct((nb, 8, Q_A), F32), jax.ShapeDtypeStruct((nb, 8, Q_B), F32)],
        grid_spec=grid_spec,
        compiler_params=_params(("arbitrary",)),
        name="sample_mix",
    )(page_table, qi, qa, qb, wi, kin, kan, van, kbn, vbn, lam4, subln, cki, cka, cva, ckb, cvb)


def _split_bf16(x):
    hi = x.astype(BF16)
    return hi, (x - hi.astype(F32)).astype(BF16)


def _outproj_kernel(alpha, oa_ref, ob_ref, sga_ref, sgb_ref, x_ref, g1_ref, sc2_ref, sh2_ref,
                    wa_ref, wb_ref, wo_ref, lg_ref, lb_ref, wrh_ref, wrl_ref, br_ref,
                    x1_ref, h2_ref, lo_ref):
    merged = (sga_ref[...].astype(F32) * _dot(oa_ref[...], wa_ref[...])
              + sgb_ref[...].astype(F32) * _dot(ob_ref[...], wb_ref[...]))
    y = _dot(merged.astype(BF16), wo_ref[...])
    x1 = _ln(alpha * x_ref[...] + (1.0 + g1_ref[0]) * y) * lg_ref[...] + lb_ref[...]
    x1_ref[...] = x1
    h2 = _ln(x1) * (1.0 + sc2_ref[0]) + sh2_ref[0]
    h2_ref[...] = h2.astype(BF16)
    hi, lo = _split_bf16(h2)
    logits = _dot(hi, wrh_ref[...]) + (_dot(lo, wrh_ref[...]) + _dot(hi, wrl_ref[...])) + br_ref[...]
    lane = lax.broadcasted_iota(I32, logits.shape, 1)
    lo_ref[...] = jnp.where(lane < N_EXPERTS, logits, NEG_INF)


def _outproj_call(oa, ob, sga, sgb, x, g1, sc2, sh2, wa, wb, wo, lg, lb, wrh, wrl, br, alpha, *, tm, mod_map):
    n = x.shape[0]
    mod_rows = g1.shape[1]
    row = lambda i: (i, 0)
    const = lambda i: (0, 0)
    mod = pl.BlockSpec((1, mod_rows, D_MODEL), mod_map)
    return pl.pallas_call(
        functools.partial(_outproj_kernel, alpha),
        out_shape=[jax.ShapeDtypeStruct((n, D_MODEL), F32), jax.ShapeDtypeStruct((n, D_MODEL), BF16),
                   jax.ShapeDtypeStruct((n, LANES), F32)],
        grid=(n // tm,),
        in_specs=[pl.BlockSpec((tm, 512), row), pl.BlockSpec((tm, 512), row),
                  pl.BlockSpec((tm, D_MODEL), row), pl.BlockSpec((tm, D_MODEL), row),
                  pl.BlockSpec((tm, D_MODEL), row), mod, mod, mod,
                  pl.BlockSpec((512, D_MODEL), const), pl.BlockSpec((512, D_MODEL), const),
                  pl.BlockSpec((D_MODEL, D_MODEL), const),
                  pl.BlockSpec((1, D_MODEL), const), pl.BlockSpec((1, D_MODEL), const),
                  pl.BlockSpec((D_MODEL, LANES), const), pl.BlockSpec((D_MODEL, LANES), const),
                  pl.BlockSpec((1, LANES), const)],
        out_specs=[pl.BlockSpec((tm, D_MODEL), row), pl.BlockSpec((tm, D_MODEL), row), pl.BlockSpec((tm, LANES), row)],
        compiler_params=_params(("arbitrary",)),
        name="outproj",
    )(oa, ob, sga, sgb, x, g1, sc2, sh2, wa, wb, wo, lg, lb, wrh, wrl, br)


def _dispatch_kernel(slots, n_real, lo_ref, h2_ref, *rest):
    xs_ref, route_ref, cnt_ref = rest[-3:]

    @pl.when(pl.program_id(0) < n_real)
    def _():
        _dispatch_block(slots, lo_ref, h2_ref, xs_ref, route_ref, cnt_ref)

    @pl.when(pl.program_id(0) >= n_real)
    def _():
        xs_ref[...] = jnp.zeros_like(xs_ref)


def _dispatch_block(slots, lo_ref, h2_ref, xs_ref, route_ref, cnt_ref):
    tb = lo_ref.shape[0]
    lane = lax.broadcasted_iota(I32, (tb, LANES), 1)
    l = lo_ref[...]
    vals, hots = [], []
    for _ in range(TOP_K_EXPERTS):
        m = jnp.max(l, axis=-1, keepdims=True)
        idx = jnp.min(jnp.where(l == m, lane, LANES), axis=-1, keepdims=True)
        hot = lane == idx
        vals.append(m)
        hots.append(hot)
        l = jnp.where(hot, NEG_INF, l)
    es = [jnp.exp(v - vals[0]) for v in vals]
    den = es[0] + es[1] + es[2] + es[3]
    ws = [e / den for e in es]
    sel = jnp.zeros((tb, LANES), F32)
    for hot in hots:
        sel = sel + jnp.where(hot, 1.0, 0.0)
    earlier = lax.broadcasted_iota(I32, (tb, tb), 1) < lax.broadcasted_iota(I32, (tb, tb), 0)
    pos = _dot(jnp.where(earlier, 1.0, 0.0).astype(BF16), sel.astype(BF16))
    cnt = jnp.sum(sel, axis=0, keepdims=True)
    run_units = jnp.right_shift(cnt.astype(I32) + (UNIT - 1), 4).astype(F32)
    lower = lax.broadcasted_iota(I32, (LANES, LANES), 0) < lax.broadcasted_iota(I32, (LANES, LANES), 1)
    off_units = _dot(jnp.broadcast_to(run_units, (8, LANES)).astype(BF16), jnp.where(lower, 1.0, 0.0).astype(BF16))
    dest = off_units[0:1] * float(UNIT) + pos
    route = jnp.zeros((tb, LANES), F32)
    for j in range(TOP_K_EXPERTS):
        dj = jnp.sum(jnp.where(hots[j], dest, 0.0), axis=-1, keepdims=True)
        route = route + jnp.where(lane == j, dj, 0.0) + jnp.where(lane == TOP_K_EXPERTS + j, ws[j], 0.0)
    route_ref[...] = route
    cnt_ref[...] = jnp.broadcast_to(cnt, (8, LANES))
    route_t = route.T
    slot = lax.broadcasted_iota(I32, (slots, tb), 0).astype(F32)
    hit = slot == route_t[0:1]
    for j in range(1, TOP_K_EXPERTS):
        hit = hit | (slot == route_t[j:j + 1])
    xs_ref[...] = _dot(jnp.where(hit, 1.0, 0.0).astype(BF16), h2_ref[...]).astype(BF16)


def _dispatch_call(logits, h2, *, tb, slots, total_blocks, blk0=0, xs_buf=None):
    n = h2.shape[0]
    nb = n // tb
    steps = nb if xs_buf is not None else total_blocks - blk0
    row = lambda i: (jnp.minimum(i, nb - 1), 0)
    args, in_specs, aliases = [logits, h2], [pl.BlockSpec((tb, LANES), row), pl.BlockSpec((tb, D_MODEL), row)], {}
    if xs_buf is not None:
        args.append(xs_buf)
        in_specs.append(pl.BlockSpec(memory_space=pl.ANY))
        aliases = {2: 0}
    return pl.pallas_call(
        functools.partial(_dispatch_kernel, slots, nb),
        out_shape=[jax.ShapeDtypeStruct((total_blocks * slots, D_MODEL), BF16), jax.ShapeDtypeStruct((n, LANES), F32),
                   jax.ShapeDtypeStruct((nb * 8, LANES), F32)],
        grid=(steps,),
        in_specs=in_specs,
        out_specs=[pl.BlockSpec((slots, D_MODEL), lambda i: (blk0 + i, 0)), pl.BlockSpec((tb, LANES), row),
                   pl.BlockSpec((8, LANES), row)],
        input_output_aliases=aliases,
        compiler_params=_params(("arbitrary",)),
        name="dispatch",
    )(*args)


def _ffn_kernel(te_ref, src_ref, dst_ref, nt_ref, xs_ref, wg_ref, bg_ref, wu_ref, bu_ref, wd_ref, bd_ref, ys_ref,
                xbuf, ybuf, wgb, wub, wdb, insem, outsem):
    j = pl.program_id(0)
    nt = nt_ref[0]
    slot = j % 2

    @pl.when((j == 0) | (te_ref[j] != te_ref[jnp.maximum(j - 1, 0)]))
    def _():
        wgb[...] = wg_ref[0].astype(BF16)
        wub[...] = wu_ref[0].astype(BF16)
        wdb[...] = wd_ref[0].astype(BF16)

    def in_copy(tile, s, k):
        return pltpu.make_async_copy(xs_ref.at[src_ref[tile * TILE_UNITS + k]],
                                     xbuf.at[s, pl.ds(k * UNIT, UNIT)], insem.at[s])

    def out_copy(tile, s, k):
        return pltpu.make_async_copy(ybuf.at[s, pl.ds(k * UNIT, UNIT)],
                                     ys_ref.at[dst_ref[tile * TILE_UNITS + k]], outsem.at[s])

    def start_in(tile, s):
        for k in range(TILE_UNITS):
            in_copy(tile, s, k).start()

    def wait_out(tile, s):
        for k in range(TILE_UNITS):
            out_copy(tile, s, k).wait()

    @pl.when((j == 0) & (nt > 0))
    def _():
        start_in(0, 0)

    @pl.when(j + 1 < nt)
    def _():
        start_in(j + 1, 1 - slot)

    @pl.when(j < nt)
    def _():
        for k in range(TILE_UNITS):
            in_copy(j, slot, k).wait()
        x = xbuf[slot]
        gate = jnp.minimum(_dot(x, wgb[...]) + bg_ref[0], SWIGLU_LIMIT)
        up = jnp.clip(_dot(x, wub[...]) + bu_ref[0], -SWIGLU_LIMIT, SWIGLU_LIMIT)
        act = (up + 1.0) * gate * jax.nn.sigmoid(SWIGLU_ALPHA * gate)
        y = _dot(act.astype(BF16), wdb[...]) + bd_ref[0]

        @pl.when(j >= 2)
        def _():
            wait_out(j - 2, slot)

        ybuf[slot] = y.astype(BF16)
        for k in range(TILE_UNITS):
            out_copy(j, slot, k).start()

    @pl.when(j == pl.num_programs(0) - 1)
    def _():
        @pl.when(nt >= 2)
        def _():
            wait_out(nt - 2, nt % 2)

        @pl.when(nt >= 1)
        def _():
            wait_out(nt - 1, (nt - 1) % 2)


def _ffn_call(tile_expert, src_units, dst_units, n_tiles, xs_units, wg, bg, wu, bu, wd, bd):
    n_units = xs_units.shape[0]
    max_tiles = tile_expert.shape[0]
    wmap = lambda j, te, src, dst, nt: (te[j], 0, 0)
    any_spec = pl.BlockSpec(memory_space=pl.ANY)
    grid_spec = pltpu.PrefetchScalarGridSpec(
        num_scalar_prefetch=4,
        grid=(max_tiles,),
        in_specs=[any_spec,
                  pl.BlockSpec((1, D_MODEL, D_FF), wmap), pl.BlockSpec((1, 1, D_FF), wmap),
                  pl.BlockSpec((1, D_MODEL, D_FF), wmap), pl.BlockSpec((1, 1, D_FF), wmap),
                  pl.BlockSpec((1, D_FF, D_MODEL), wmap), pl.BlockSpec((1, 1, D_MODEL), wmap)],
        out_specs=any_spec,
        scratch_shapes=[pltpu.VMEM((2, TILE_M, D_MODEL), BF16), pltpu.VMEM((2, TILE_M, D_MODEL), BF16),
                        pltpu.VMEM((D_MODEL, D_FF), BF16), pltpu.VMEM((D_MODEL, D_FF), BF16),
                        pltpu.VMEM((D_FF, D_MODEL), BF16),
                        pltpu.SemaphoreType.DMA((2,)), pltpu.SemaphoreType.DMA((2,))],
    )
    return pl.pallas_call(
        _ffn_kernel,
        out_shape=jax.ShapeDtypeStruct((n_units, UNIT, D_MODEL), BF16),
        grid_spec=grid_spec,
        input_output_aliases={4: 0},
        compiler_params=_params(("arbitrary",)),
        name="ffn",
    )(tile_expert, src_units, dst_units, n_tiles, xs_units, wg, bg, wu, bu, wd, bd)


def _unit_tables(cnt, units_per_block, max_tiles, spare0):
    nb = cnt.shape[0]
    run_units = (cnt + (UNIT - 1)) // UNIT
    blk_off = jnp.cumsum(run_units, axis=1) - run_units
    cum = jnp.cumsum(run_units, axis=0)
    tot = cum[-1]
    tiles = (tot + (TILE_UNITS - 1)) // TILE_UNITS
    tile_end = jnp.cumsum(tiles)
    n_tiles = tile_end[-1]
    j = jnp.arange(max_tiles, dtype=I32)
    jc = jnp.minimum(j, n_tiles - 1)
    te = jnp.minimum(jnp.sum(tile_end[None, :] <= jc[:, None], axis=1), N_EXPERTS - 1).astype(I32)
    tile_start = (tile_end - tiles)[te]
    rank = ((j - tile_start) * TILE_UNITS)[:, None] + jnp.arange(TILE_UNITS, dtype=I32)[None, :]
    cum_t = cum.T[te]
    blk = jnp.sum(cum_t[:, None, :] <= rank[:, :, None], axis=-1).astype(I32)
    blk_c = jnp.minimum(blk, nb - 1)
    before = jnp.take_along_axis(cum_t - run_units.T[te], blk_c, axis=1)
    unit = blk_c * units_per_block + blk_off[blk_c, te[:, None]] + (rank - before)
    valid = (rank < tot[te][:, None]) & (j < n_tiles)[:, None]
    k = jnp.arange(TILE_UNITS, dtype=I32)[None, :]
    src = jnp.where(valid, unit, spare0 + k).astype(I32).reshape(-1)
    dst = jnp.where(valid, unit, spare0 + TILE_UNITS * (1 + j[:, None] % 2) + k).astype(I32).reshape(-1)
    return te, src, dst, n_tiles.astype(I32).reshape(1)


def _combine_kernel(alpha, ys_ref, route_ref, x1_ref, g2_ref, lg_ref, lb_ref, y_ref):
    tb = x1_ref.shape[0]
    slots = ys_ref.shape[0]
    route = route_ref[...]
    slot = lax.broadcasted_iota(I32, (tb, slots), 1).astype(F32)
    pw = jnp.zeros((tb, slots), F32)
    for j in range(TOP_K_EXPERTS):
        pw = pw + jnp.where(slot == route[:, j:j + 1], route[:, TOP_K_EXPERTS + j:TOP_K_EXPERTS + j + 1], 0.0)
    moe = _dot(pw.astype(BF16), ys_ref[...])
    y_ref[...] = _ln(alpha * x1_ref[...] + (1.0 + g2_ref[0]) * moe) * lg_ref[...] + lb_ref[...]


def _combine_call(ys, route, x1, g2, lg, lb, alpha, *, tb, slots, blk0, mod_map):
    n = x1.shape[0]
    mod_rows = g2.shape[1]
    row = lambda i: (i, 0)
    const = lambda i: (0, 0)
    return pl.pallas_call(
        functools.partial(_combine_kernel, alpha),
        out_shape=jax.ShapeDtypeStruct((n, D_MODEL), F32),
        grid=(n // tb,),
        in_specs=[pl.BlockSpec((slots, D_MODEL), lambda i: (blk0 + i, 0)),
                  pl.BlockSpec((tb, LANES), row), pl.BlockSpec((tb, D_MODEL), row),
                  pl.BlockSpec((1, mod_rows, D_MODEL), mod_map),
                  pl.BlockSpec((1, D_MODEL), const), pl.BlockSpec((1, D_MODEL), const)],
        out_specs=pl.BlockSpec((tb, D_MODEL), row),
        compiler_params=_params(("arbitrary",)),
        name="combine",
    )(ys, route, x1, g2, lg, lb)


def _rope_tables(pos):
    half = HEAD_DIM // 2
    inv = ROPE_THETA ** (-jnp.arange(half, dtype=F32) / half)
    ang = jnp.asarray(pos).astype(F32)[:, None] * inv[None, :]
    cos, sin = jnp.cos(ang), jnp.sin(ang)
    return (jnp.tile(cos, (1, LANES // half)),
            jnp.tile(jnp.concatenate([-sin, sin], axis=1), (1, LANES // HEAD_DIM)))


def _prep_w_in(w):
    offs = np.concatenate([[0], np.cumsum(PROJ_SIZES)])
    qa, ka, va, qi, ki, wi, qb, kb, vb, ga, gb = [w[:, offs[i]:offs[i + 1]] for i in range(11)]
    pad = jnp.zeros((D_MODEL, LANES - IDX_DIM - N_IDX_HEADS), w.dtype)
    return jnp.concatenate([qa * QK_SCALE, qi, qb * QK_SCALE, ka, va, kb, vb, ki, wi, pad, ga, gb], axis=1).astype(BF16)


def _block_diag_rows(q, n_blocks):
    eye = jnp.eye(n_blocks, dtype=q.dtype)
    nb, _, rows, d = q.shape
    return jnp.einsum("bgrd,gh->bgrhd", q, eye).reshape(nb, n_blocks * rows, n_blocks * d)


def kernel(x_prompt, x_sample, cache_k_a, cache_v_a, cache_k_idx, cache_k_b, cache_v_b, page_table, c_prompt, c_sample, w_ada, b_ada, w_in, w_br_a, w_br_b, w_o, lambda_q1, lambda_k1, lambda_q2, lambda_k2, subln_g, ln1_g, ln1_b, ln2_g, ln2_b, w_router, b_router, w_gate, b_gate, w_up, b_up, w_down, b_down):
    nbp, seq, _ = x_prompt.shape
    nbs, dseq, _ = x_sample.shape
    depth = w_ada.shape[0]
    assert depth == 1 and dseq == 8
    n_pages = page_table.shape[1]
    past = n_pages * PAGE
    alpha = (2 * depth) ** 0.25
    lam_init = 0.8 - 0.6 * math.exp(-0.3 * 0)
    np_tok, ns_tok = nbp * seq, nbs * dseq

    tm = min(512, seq)
    tq = min(256, seq)
    tb = tm
    slots = ((4 * tb + N_EXPERTS * (UNIT - 1) + TILE_M - 1) // TILE_M) * TILE_M
    upb = slots // UNIT
    ppc = min(16, n_pages // 2)

    w_in_r = _prep_w_in(w_in[0])
    wa, wb, wo = w_br_a[0].astype(BF16), w_br_b[0].astype(BF16), w_o[0].astype(BF16)
    wr = jnp.pad(w_router[0], ((0, 0), (0, LANES - N_EXPERTS)))
    wrh = wr.astype(BF16)
    wrl = (wr - wrh.astype(F32)).astype(BF16)
    br = jnp.pad(b_router[0], (0, LANES - N_EXPERTS)).reshape(1, LANES)
    wg, wu, wd = w_gate[0], w_up[0], w_down[0]
    bg, bu, bd = b_gate[0][:, None, :], b_up[0][:, None, :], b_down[0][:, None, :]
    lam4 = jnp.stack([lambda_q1[0], lambda_k1[0], lambda_q2[0], lambda_k2[0]])
    subln = subln_g[0].reshape(1, -1)
    l1g, l1b, l2g, l2b = (a[0].reshape(1, -1) for a in (ln1_g, ln1_b, ln2_g, ln2_b))

    mod = _ada_call(jnp.concatenate([c_prompt, c_sample], axis=0), w_ada[0], b_ada[0])
    mod_p = [m.reshape(nbp, 1, D_MODEL) for m in jnp.split(mod[:nbp], 6, axis=-1)]
    mod_s = [jnp.repeat(m, dseq, axis=0).reshape(1, ns_tok, D_MODEL) for m in jnp.split(mod[nbp:], 6, axis=-1)]
    tiles_per_seq = seq // tm
    pmap = lambda i: (i // tiles_per_seq, 0, 0)
    smap = lambda i: (0, 0, 0)

    cos_p, sin_p = _rope_tables(np.arange(seq))
    cos_s, sin_s = _rope_tables(np.tile(past + np.arange(dseq), nbs))
    xp = x_prompt.reshape(np_tok, D_MODEL)
    xs = x_sample.reshape(ns_tok, D_MODEL)
    names = [n for n, _, _ in _PROJ_OUTS]
    pp = dict(zip(names, _proj_call(xp, mod_p[1], mod_p[0], w_in_r, cos_p, sin_p, tm=tm, mod_map=pmap,
                                    pos_map=lambda i: (i % tiles_per_seq, 0), seq=seq)))
    ps = dict(zip(names, _proj_call(xs, mod_s[1], mod_s[0], w_in_r, cos_s, sin_s, tm=ns_tok, mod_map=smap,
                                    pos_map=lambda i: (0, 0))))

    oa_p, ob_p = _attn_call(pp, lam4, subln, nbp, seq, lam_init, tq=tq)

    def rows(q, n_heads):
        return q.reshape(nbs, dseq, n_heads, HEAD_DIM).transpose(0, 2, 1, 3)

    qi_s = rows(ps["qi"], N_IDX_HEADS).reshape(nbs, N_IDX_HEADS * dseq, IDX_DIM)
    qa_s = _block_diag_rows(rows(ps["qa"], N_HEADS_A).reshape(nbs, N_KV_A, (N_HEADS_A // N_KV_A) * dseq, HEAD_DIM), N_KV_A)
    qb_r = rows(ps["qb"], 2 * N_HEADS_B)
    rb = N_HEADS_B // N_KV_B
    qb_r = qb_r.reshape(nbs, N_KV_B, rb, 2, dseq, HEAD_DIM)
    eye2 = jnp.eye(2, dtype=BF16)
    qb_s = jnp.einsum("bgrcqd,gh,ce->bgrcqhed", qb_r, eye2, eye2).reshape(nbs, N_KV_B * rb * 2 * dseq, N_KV_B * 2 * HEAD_DIM)

    def new_rows(a):
        return jnp.pad(a.reshape(nbs, dseq, -1), ((0, 0), (0, LANES - dseq), (0, 0)))

    oa_s, ob_s = _sample_call(
        page_table, qi_s, qa_s, qb_s, ps["kiwi"].reshape(nbs, dseq, LANES),
        new_rows(ps["kib"][:, :IDX_DIM]), new_rows(ps["kab"]), new_rows(ps["vab"]), new_rows(ps["kbb"]), new_rows(ps["vbb"]),
        lam4, subln,
        jnp.transpose(cache_k_idx[0], (0, 2, 1)),
        jnp.transpose(cache_k_a[0], (0, 2, 3, 1)).reshape(-1, KV_A, PAGE),
        jnp.transpose(cache_v_a[0], (0, 2, 3, 1)).reshape(-1, KV_A, PAGE),
        jnp.transpose(cache_k_b[0], (0, 2, 3, 4, 1)).reshape(-1, KV_B, PAGE),
        cache_v_b[0].reshape(-1, PAGE * N_KV_B, 2 * HEAD_DIM), lam_init, ppc=ppc)
    oa_s = oa_s.reshape(ns_tok, Q_A).astype(BF16)
    ob_s = ob_s.reshape(ns_tok, Q_B).astype(BF16)

    x1_p, h2_p, lg_p = _outproj_call(oa_p, ob_p, pp["sga"], pp["sgb"], xp, mod_p[2], mod_p[4], mod_p[3],
                                     wa, wb, wo, l1g, l1b, wrh, wrl, br, alpha, tm=tm, mod_map=pmap)
    x1_s, h2_s, lg_s = _outproj_call(oa_s, ob_s, ps["sga"], ps["sgb"], xs, mod_s[2], mod_s[4], mod_s[3],
                                     wa, wb, wo, l1g, l1b, wrh, wrl, br, alpha, tm=ns_tok, mod_map=smap)

    nblk_p = np_tok // tb
    n_blocks = nblk_p + 2
    assert upb >= 3 * TILE_UNITS
    xs_all, route_p, cnt_p = _dispatch_call(lg_p, h2_p, tb=tb, slots=slots, total_blocks=n_blocks)
    xs_all, route_s, cnt_s = _dispatch_call(lg_s, h2_s, tb=ns_tok, slots=slots, total_blocks=n_blocks,
                                            blk0=nblk_p, xs_buf=xs_all)
    cnt = jnp.concatenate([cnt_p.reshape(nblk_p, 8, LANES)[:, 0, :N_EXPERTS],
                           cnt_s.reshape(1, 8, LANES)[:, 0, :N_EXPERTS]], axis=0).astype(I32)
    total_units = (np_tok + ns_tok) * TOP_K_EXPERTS // UNIT + (nblk_p + 1) * N_EXPERTS
    max_tiles = total_units // TILE_UNITS + N_EXPERTS
    te, src, dst, n_tiles = _unit_tables(cnt, upb, max_tiles, (nblk_p + 1) * upb)
    ys = _ffn_call(te, src, dst, n_tiles, xs_all.reshape(-1, UNIT, D_MODEL), wg, bg, wu, bu, wd, bd).reshape(-1, D_MODEL)
    y_p = _combine_call(ys, route_p, x1_p, mod_p[5], l2g, l2b, alpha, tb=tb, slots=slots, blk0=0, mod_map=pmap)
    y_s = _combine_call(ys, route_s, x1_s, mod_s[5], l2g, l2b, alpha, tb=ns_tok, slots=slots, blk0=nblk_p, mod_map=smap)

    st = lambda a, nb_, t_, shp: a.reshape((1, nb_, t_) + shp)

    def st_t(a, shp):
        a = a.reshape((nbp,) + shp + (seq,))
        return jnp.transpose(a, (0, a.ndim - 1) + tuple(range(1, a.ndim - 1)))[None]

    outs = [y_p.reshape(nbp, seq, D_MODEL), y_s.reshape(nbs, dseq, D_MODEL),
            st_t(pp["ka"], (N_KV_A, HEAD_DIM)), st_t(pp["va"], (N_KV_A, HEAD_DIM)), st_t(pp["ki"], (IDX_DIM,)),
            st_t(pp["kb"], (N_KV_B, 2, HEAD_DIM)), st(pp["vb"], nbp, seq, (N_KV_B, 2 * HEAD_DIM)),
            st(ps["ka"], nbs, dseq, (N_KV_A, HEAD_DIM)), st(ps["va"], nbs, dseq, (N_KV_A, HEAD_DIM)),
            st(ps["ki"], nbs, dseq, (IDX_DIM,)), st(ps["kb"], nbs, dseq, (N_KV_B, 2, HEAD_DIM)),
            st(ps["vb"], nbs, dseq, (N_KV_B, 2 * HEAD_DIM))]
    return tuple(outs)
```
